```python
import functools
import jax, jax.numpy as jnp
from jax import lax
import numpy as np

D_MODEL = 2048
BATCH = 2
SEQ = 4096
DEPTH = 4
DEC_BATCH = 128
DEC_SEQ = 8
PAST_LEN = 8192
PAGE_SIZE = 128

N_A_LAYERS = DEPTH // 2
N_B_LAYERS = DEPTH - N_A_LAYERS
CHUNK = 128
GMLP_GROUPS = 16
D_GMLP = 3 * D_MODEL
GMLP_GROUP_DIM = D_GMLP // GMLP_GROUPS
N_HEADS = 16
QK_NOPE = 128
QK_ROPE = 64
V_HEAD = 128
KV_LORA = 512
Q_LORA = 512
ROPE_THETA = 10000.0
Q_BLOCK = 128
SM_SCALE = (QK_NOPE + QK_ROPE) ** -0.5
D_FF = 5632
CONV_W = 3
ALPHA = (2 * DEPTH) ** 0.25
BETA = (8 * DEPTH) ** -0.25
EPS = 1e-5

kernel_name = 'yoco_gmlp_mla_convffn_step'


def _layer_norm(x, g, b):
    xf = x.astype(jnp.float32)
    mu = xf.mean(-1, keepdims=True)
    var = jnp.square(xf - mu).mean(-1, keepdims=True)
    return ((xf - mu) * lax.rsqrt(var + EPS)).astype(x.dtype) * g + b


def _rms_norm(x, g):
    xf = x.astype(jnp.float32)
    return (xf * lax.rsqrt(jnp.square(xf).mean(-1, keepdims=True) + EPS)).astype(x.dtype) * g


def _rope(x, pos):
    half = QK_ROPE // 2
    inv = jnp.power(ROPE_THETA, -jnp.arange(half, dtype=jnp.float32) / half)
    ang = pos.astype(jnp.float32)[:, None] * inv[None, :]
    shape = (pos.shape[0],) + (1,) * (x.ndim - 3) + (half,)
    cos = jnp.cos(ang).reshape(shape)
    sin = jnp.sin(ang).reshape(shape)
    xf = x.astype(jnp.float32)
    x1, x2 = xf[..., :half], xf[..., half:]
    return jnp.concatenate([x1 * cos - x2 * sin, x2 * cos + x1 * sin], -1).astype(x.dtype)


def _modulation(c, w, b, n):
    m = (jax.nn.silu(c) @ w + b).reshape(c.shape[0], n, 1, D_MODEL)
    return [m[:, i] for i in range(n)]


def _spatial_gate(v, w_s, b_s):
    bsz, t, _ = v.shape
    n_chunks = -(-t // CHUNK)
    pad = n_chunks * CHUNK - t
    vp = jnp.pad(v, ((0, 0), (0, pad), (0, 0))).reshape(bsz, n_chunks, CHUNK, GMLP_GROUPS, GMLP_GROUP_DIM)
    w = w_s * jnp.tril(jnp.ones((CHUNK, CHUNK), w_s.dtype))
    s = jnp.einsum('gts,bcsgd->bctgd', w, vp) + b_s.T[None, None, :, :, None]
    return s.reshape(bsz, n_chunks * CHUNK, D_GMLP)[:, :t]


def _gmlp(h, w_in, b_in, ln_g, ln_b, w_s, b_s, w_out):
    z = jax.nn.gelu(h @ w_in + b_in, approximate=False)
    u, v = jnp.split(z, 2, axis=-1)
    v = _layer_norm(v, ln_g, ln_b)
    return (u * _spatial_gate(v, w_s, b_s)) @ w_out, v


def _conv_ffn(h, prev, w_up, conv_w, conv_b, w_down):
    g, val = jnp.split(h @ w_up, 2, axis=-1)
    t = g.shape[1]
    gp = jnp.concatenate([prev.astype(g.dtype), g], axis=1)
    gc = conv_b
    for k in range(CONV_W):
        gc = gc + conv_w[k] * gp[:, k:k + t]
    return (jax.nn.silu(gc) * val) @ w_down, gp[:, t:]


def _prepare_full(latent, k_rope, w_uk, w_uv):
    k_nope = jnp.einsum('bsc,chn->bshn', latent, w_uk)
    v = jnp.einsum('bsc,chd->bshd', latent, w_uv)
    return (k_nope, k_rope, v)


def _attend_full(ctx, q_nope, q_rope):
    k_nope, k_rope, v = ctx
    bsz, s, h, _ = q_nope.shape
    nb = s // Q_BLOCK
    kpos = jnp.arange(s)

    def blocks(a):
        return a.reshape((bsz, nb, Q_BLOCK) + a.shape[2:]).swapaxes(0, 1)

    def one(args):
        qn, qr, i = args
        sc = jnp.einsum('bqhn,bkhn->bhqk', qn, k_nope) + jnp.einsum('bqhr,bkr->bhqk', qr, k_rope)
        sc = sc.astype(jnp.float32) * SM_SCALE
        qpos = i * Q_BLOCK + jnp.arange(Q_BLOCK)
        sc = jnp.where(qpos[:, None] >= kpos[None, :], sc, -jnp.inf)
        p = jax.nn.softmax(sc, axis=-1).astype(v.dtype)
        return jnp.einsum('bhqk,bkhd->bqhd', p, v)

    o = lax.map(one, (blocks(q_nope), blocks(q_rope), jnp.arange(nb)))
    return o.swapaxes(0, 1).reshape(bsz, s, h, V_HEAD)


def _prepare_latent(latent, k_rope):
    return (latent, k_rope)


def _attend_paged(ctx, q_nope, q_rope, cache_lat, cache_kr, page_table, w_uk, w_uv):
    lat_new, kr_new = ctx
    t = q_nope.shape[1]
    past = page_table.shape[1] * cache_lat.shape[1]
    q_lat = jnp.einsum('bthn,chn->bthc', q_nope, w_uk)
    kpos = jnp.arange(past + t)
    qpos = past + jnp.arange(t)
    mask = kpos[None, :] <= qpos[:, None]

    def one(args):
        ql, qr, pt, ln, kn = args
        keys_c = jnp.concatenate([cache_lat[pt].reshape(past, KV_LORA).astype(ln.dtype), ln], 0)
        keys_r = jnp.concatenate([cache_kr[pt].reshape(past, QK_ROPE).astype(kn.dtype), kn], 0)
        sc = jnp.einsum('thc,kc->htk', ql, keys_c) + jnp.einsum('thr,kr->htk', qr, keys_r)
        sc = jnp.where(mask[None], sc.astype(jnp.float32) * SM_SCALE, -jnp.inf)
        p = jax.nn.softmax(sc, axis=-1).astype(keys_c.dtype)
        return jnp.einsum('htk,kc->thc', p, keys_c)

    o_lat = lax.map(one, (q_lat, q_rope, page_table, lat_new, kr_new))
    return jnp.einsum('bthc,chd->bthd', o_lat, w_uv)


def _trunk(x, c, pos, conv_prev, prepare_kv, attend, W):
    conv_new, v_rows = [], []
    kv_ctx, latent, k_rope = None, None, None
    for l in range(DEPTH):
        sh1, sc1, g1, sh2, sc2, g2 = _modulation(c, W['ada_w'][l], W['ada_b'][l], 6)
        h = x * (1 + sc1) + sh1
        if l < N_A_LAYERS:
            mix, v = _gmlp(h, W['gm_w_in'][l], W['gm_b_in'][l], W['gm_ln_g'][l], W['gm_ln_b'][l],
                           W['gm_w_s'][l], W['gm_b_s'][l], W['gm_w_out'][l])
            v_rows.append(v)
        else:
            j = l - N_A_LAYERS
            cq = _rms_norm(h @ W['w_dq'][j], W['q_norm_g'][j])
            q = jnp.einsum('btc,chn->bthn', cq, W['w_uq'][j])
            q_nope = q[..., :QK_NOPE]
            q_rope = _rope(q[..., QK_NOPE:], pos)
            o = attend(kv_ctx, q_nope, q_rope)
            mix = jnp.einsum('bthd,hdm->btm', o, W['w_o'][j])
        x = _layer_norm(ALPHA * x + g1 * mix, W['post_ln_g'][l, 0], W['post_ln_b'][l, 0])
        h = x * (1 + sc2) + sh2
        f, tail = _conv_ffn(h, conv_prev[l], W['ffn_w_up'][l], W['ffn_conv_w'][l],
                            W['ffn_conv_b'][l], W['ffn_w_down'][l])
        conv_new.append(tail)
        x = _layer_norm(ALPHA * x + g2 * f, W['post_ln_g'][l, 1], W['post_ln_b'][l, 1])
        if l == N_A_LAYERS - 1:
            sh_kv, sc_kv = _modulation(c, W['kv_ada_w'], W['kv_ada_b'], 2)
            a = (x * (1 + sc_kv) + sh_kv) @ W['w_dkv']
            latent = _rms_norm(a[..., :KV_LORA], W['kv_norm_g'])
            k_rope = _rope(a[..., KV_LORA:], pos)
            kv_ctx = prepare_kv(latent, k_rope)
    return x, latent, k_rope, jnp.stack(conv_new), jnp.stack(v_rows)


def setup_inputs(seed: int = 0) -> dict:
    key = jax.random.key(seed)
    ks = iter(jax.random.split(key, 40))

    def nrm(shape, scale):
        return jax.random.normal(next(ks), shape, jnp.float32) * scale

    D = D_MODEL
    n_pages = PAST_LEN // PAGE_SIZE
    n_pool = (DEC_BATCH * n_pages * 5) // 4
    x_prompt = nrm((BATCH, SEQ, D), 1.0)
    x_sample = nrm((DEC_BATCH, DEC_SEQ, D), 1.0)
    cache_kv_latent = nrm((n_pool, PAGE_SIZE, KV_LORA), 1.0)
    cache_k_rope = nrm((n_pool, PAGE_SIZE, QK_ROPE), 1.0)
    state_conv = nrm((DEPTH, DEC_BATCH, CONV_W - 1, D_FF), 1.0)
    page_table = jax.random.permutation(next(ks), n_pool)[:DEC_BATCH * n_pages].reshape(
        DEC_BATCH, n_pages).astype(jnp.int32)
    c_prompt = nrm((BATCH, D), 1.0)
    c_sample = nrm((DEC_BATCH, D), 1.0)
    return {
        'x_prompt': x_prompt,
        'x_sample': x_sample,
        'cache_kv_latent': cache_kv_latent,
        'cache_k_rope': cache_k_rope,
        'state_conv': state_conv,
        'page_table': page_table,
        'c_prompt': c_prompt,
        'c_sample': c_sample,
        'ada_w': nrm((DEPTH, D, 6 * D), 0.5 * D ** -0.5),
        'ada_b': nrm((DEPTH, 6 * D), 0.02),
        'post_ln_g': 1.0 + nrm((DEPTH, 2, D), 0.02),
        'post_ln_b': nrm((DEPTH, 2, D), 0.02),
        'gm_w_in': nrm((N_A_LAYERS, D, 2 * D_GMLP), D ** -0.5),
        'gm_b_in': nrm((N_A_LAYERS, 2 * D_GMLP), 0.02),
        'gm_ln_g': 1.0 + nrm((N_A_LAYERS, D_GMLP), 0.02),
        'gm_ln_b': nrm((N_A_LAYERS, D_GMLP), 0.02),
        'gm_w_s': nrm((N_A_LAYERS, GMLP_GROUPS, CHUNK, CHUNK), CHUNK ** -0.5),
        'gm_b_s': 1.0 + nrm((N_A_LAYERS, GMLP_GROUPS, CHUNK), 0.1),
        'gm_w_out': nrm((N_A_LAYERS, D_GMLP, D), BETA * D_GMLP ** -0.5),
        'kv_ada_w': nrm((D, 2 * D), 0.5 * D ** -0.5),
        'kv_ada_b': nrm((2 * D,), 0.02),
        'w_dkv': nrm((D, KV_LORA + QK_ROPE), D ** -0.5),
        'kv_norm_g': 1.0 + nrm((KV_LORA,), 0.02),
        'w_uk': nrm((KV_LORA, N_HEADS, QK_NOPE), KV_LORA ** -0.5),
        'w_uv': nrm((KV_LORA, N_HEADS, V_HEAD), KV_LORA ** -0.5),
        'w_dq': nrm((N_B_LAYERS, D, Q_LORA), D ** -0.5),
        'q_norm_g': 1.0 + nrm((N_B_LAYERS, Q_LORA), 0.02),
        'w_uq': nrm((N_B_LAYERS, Q_LORA, N_HEADS, QK_NOPE + QK_ROPE), Q_LORA ** -0.5),
        'w_o': nrm((N_B_LAYERS, N_HEADS, V_HEAD, D), BETA * (N_HEADS * V_HEAD) ** -0.5),
        'ffn_w_up': nrm((DEPTH, D, 2 * D_FF), D ** -0.5),
        'ffn_conv_w': nrm((DEPTH, CONV_W, D_FF), CONV_W ** -0.5),
        'ffn_conv_b': nrm((DEPTH, D_FF), 0.02),
        'ffn_w_down': nrm((DEPTH, D_FF, D), BETA * D_FF ** -0.5),
    }


def reference(x_prompt, x_sample, cache_kv_latent, cache_k_rope, state_conv, page_table, c_prompt, c_sample,
              ada_w, ada_b, post_ln_g, post_ln_b,
              gm_w_in, gm_b_in, gm_ln_g, gm_ln_b, gm_w_s, gm_b_s, gm_w_out,
              kv_ada_w, kv_ada_b, w_dkv, kv_norm_g, w_uk, w_uv,
              w_dq, q_norm_g, w_uq, w_o,
              ffn_w_up, ffn_conv_w, ffn_conv_b, ffn_w_down):
    W = dict(ada_w=ada_w, ada_b=ada_b, post_ln_g=post_ln_g, post_ln_b=post_ln_b,
             gm_w_in=gm_w_in, gm_b_in=gm_b_in, gm_ln_g=gm_ln_g, gm_ln_b=gm_ln_b,
             gm_w_s=gm_w_s, gm_b_s=gm_b_s, gm_w_out=gm_w_out,
             kv_ada_w=kv_ada_w, kv_ada_b=kv_ada_b, w_dkv=w_dkv, kv_norm_g=kv_norm_g,
             w_dq=w_dq, q_norm_g=q_norm_g, w_uq=w_uq, w_o=w_o,
             ffn_w_up=ffn_w_up, ffn_conv_w=ffn_conv_w, ffn_conv_b=ffn_conv_b, ffn_w_down=ffn_w_down)

    seq_p = x_prompt.shape[1]
    pos_p = jnp.arange(seq_p, dtype=jnp.int32)
    conv0 = jnp.zeros((DEPTH, x_prompt.shape[0], CONV_W - 1, D_FF), x_prompt.dtype)
    y_prompt, lat_p, kr_p, conv_p, _ = _trunk(
        x_prompt, c_prompt, pos_p, conv0,
        functools.partial(_prepare_full, w_uk=w_uk, w_uv=w_uv), _attend_full, W)

    past = page_table.shape[1] * PAGE_SIZE
    pos_s = past + jnp.arange(x_sample.shape[1], dtype=jnp.int32)
    attend_s = functools.partial(_attend_paged, cache_lat=cache_kv_latent, cache_kr=cache_k_rope,
                                 page_table=page_table, w_uk=w_uk, w_uv=w_uv)
    y_sample, lat_s, kr_s, conv_s, v_s = _trunk(
        x_sample, c_sample, pos_s, state_conv, _prepare_latent, attend_s, W)

    return (y_prompt, y_sample, lat_p, kr_p, lat_s, kr_s, conv_p, conv_s, v_s)
```

```python
import functools

import numpy as np
import jax
import jax.numpy as jnp
from jax import lax
from jax.experimental import pallas as pl
from jax.experimental.pallas import tpu as pltpu

F32 = jnp.float32
BF = jnp.bfloat16

D_MODEL = 2048
DEPTH = 4
N_A_LAYERS = DEPTH // 2
CHUNK = 128
GMLP_GROUPS = 16
D_GMLP = 3 * D_MODEL
GROUP_DIM = D_GMLP // GMLP_GROUPS
N_HEADS = 16
QK_NOPE = 128
QK_ROPE = 64
V_HEAD = 128
KV_LORA = 512
Q_LORA = 512
ROPE_THETA = 10000.0
SM_SCALE = (QK_NOPE + QK_ROPE) ** -0.5
D_FF = 5632
CONV_W = 3
ALPHA = (2 * DEPTH) ** 0.25
EPS = 1e-5
HEAD_W = 2 * QK_NOPE
PAGES_PER_STEP = 8
VMEM_LIMIT = 56 * 1024 * 1024


def _cparams(n_axes):
    return pltpu.CompilerParams(dimension_semantics=("arbitrary",) * n_axes,
                                vmem_limit_bytes=VMEM_LIMIT)


class RowVec:
    def __init__(self, v, t_rows):
        self.b, self.d = v.shape
        self.t = t_rows
        self.expanded = t_rows < CHUNK
        self.arr = jnp.repeat(v, t_rows, axis=0) if self.expanded else v.reshape(self.b, 1, self.d)

    def spec(self, tm, row_tile):
        if self.expanded:
            return pl.BlockSpec((tm, self.d), lambda *g: (row_tile(*g), 0))
        t = self.t
        return pl.BlockSpec((None, 1, self.d), lambda *g: ((row_tile(*g) * tm) // t, 0, 0))


def _modulate_kernel(x_ref, sc_ref, sh_ref, o_ref):
    o_ref[...] = (x_ref[...] * (1.0 + sc_ref[...]) + sh_ref[...]).astype(o_ref.dtype)


def modulate(x, sc, sh, tm):
    r, d = x.shape
    row = lambda i: i
    return pl.pallas_call(
        _modulate_kernel,
        grid=(r // tm,),
        in_specs=[pl.BlockSpec((tm, d), lambda i: (i, 0)), sc.spec(tm, row), sh.spec(tm, row)],
        out_specs=pl.BlockSpec((tm, d), lambda i: (i, 0)),
        out_shape=jax.ShapeDtypeStruct((r, d), BF),
        compiler_params=_cparams(1),
        name="modulate",
    )(x, sc.arr, sh.arr)


def _mm_kernel(*refs, x_act, has_bias, act, has_rms, has_tab, sub):
    it = iter(refs)
    x_ref, w_ref = next(it), next(it)
    b_ref = next(it) if has_bias else None
    g_ref = next(it) if has_rms else None
    t_ref = next(it) if has_tab else None
    o_ref, wb = next(it), next(it)

    @pl.when(pl.program_id(1) == 0)
    def _():
        wb[...] = w_ref[...].astype(BF)

    xv = x_ref[...]
    if x_act == "silu":
        xf = xv.astype(F32)
        xv = xf * jax.nn.sigmoid(xf)
    xb = xv.astype(BF)
    tn = o_ref.shape[1]
    for c in range(tn // sub):
        sl = slice(c * sub, (c + 1) * sub)
        acc = jnp.dot(xb, wb[:, sl], preferred_element_type=F32)
        if has_bias:
            acc = acc + b_ref[:, sl]
        if act == "gelu":
            acc = 0.5 * acc * (1.0 + lax.erf(acc * np.float32(2.0 ** -0.5)))
        if has_rms:
            acc = acc * lax.rsqrt(jnp.mean(acc * acc, axis=-1, keepdims=True) + EPS) * g_ref[...]
        if has_tab:
            acc = acc * t_ref[...]
        o_ref[:, sl] = acc.astype(o_ref.dtype)


def mm(x, w, l, *, bias=None, x_act=None, act=None, rms_g=None, tab=None, tab_idx=None,
       out_dtype=BF, tm, tn, sub=256):
    m, k = x.shape
    n = w.shape[2]
    sub = min(sub, tn)
    if rms_g is not None:
        assert tn == n
        sub = tn
    in_specs = [pl.BlockSpec((tm, k), lambda j, i: (i, 0)),
                pl.BlockSpec((None, k, tn), lambda j, i: (l, 0, j))]
    args = [x, w]
    for v in (bias, rms_g):
        if v is not None:
            in_specs.append(pl.BlockSpec((None, 1, tn), lambda j, i: (l, 0, j)))
            args.append(v.reshape(v.shape[0], 1, n))
    if tab is not None:
        assert tab.shape[1] == sub
        in_specs.append(pl.BlockSpec((tm, sub), lambda j, i: (tab_idx(i), 0)))
        args.append(tab)
    kern = functools.partial(_mm_kernel, x_act=x_act, has_bias=bias is not None, act=act,
                             has_rms=rms_g is not None, has_tab=tab is not None, sub=sub)
    return pl.pallas_call(
        kern,
        grid=(n // tn, m // tm),
        in_specs=in_specs,
        out_specs=pl.BlockSpec((tm, tn), lambda j, i: (i, j)),
        out_shape=jax.ShapeDtypeStruct((m, n), out_dtype),
        scratch_shapes=[pltpu.VMEM((k, tn), BF)],
        compiler_params=_cparams(2),
        name="mm",
    )(*args)


def _gate_kernel(u_ref, v_ref, lg_ref, lb_ref, w_ref, bt_ref, *out_refs, block_diag, write_v):
    o_ref = out_refs[0]
    v = v_ref[...].astype(F32)
    mu = jnp.mean(v, axis=-1, keepdims=True)
    dv = v - mu
    var = jnp.mean(dv * dv, axis=-1, keepdims=True)
    vn = dv * lax.rsqrt(var + EPS) * lg_ref[...] + lb_ref[...]
    if write_v:
        out_refs[1][...] = vn
    vb = vn.astype(BF)
    row = lax.broadcasted_iota(jnp.int32, (CHUNK, CHUNK), 0)
    col = lax.broadcasted_iota(jnp.int32, (CHUNK, CHUNK), 1)
    mask = row >= col
    if block_diag:
        mask = jnp.logical_and(mask, (row // block_diag) == (col // block_diag))
    for g in range(GMLP_GROUPS):
        sl = slice(g * GROUP_DIM, (g + 1) * GROUP_DIM)
        wm = jnp.where(mask, w_ref[g], 0.0).astype(BF)
        s = jnp.dot(wm, vb[:, sl], preferred_element_type=F32) + bt_ref[:, g:g + 1]
        o_ref[:, sl] = (u_ref[:, sl].astype(F32) * s).astype(o_ref.dtype)


def spatial_gate(z, ln_g, ln_b, l, w_mix, b_t, *, block_diag, write_v):
    r = z.shape[0]
    nb = D_GMLP // D_GMLP
    out_shape = [jax.ShapeDtypeStruct((r, D_GMLP), BF)]
    out_specs = [pl.BlockSpec((CHUNK, D_GMLP), lambda c: (c, 0))]
    if write_v:
        out_shape.append(jax.ShapeDtypeStruct((r, D_GMLP), F32))
        out_specs.append(pl.BlockSpec((CHUNK, D_GMLP), lambda c: (c, 0)))
    res = pl.pallas_call(
        functools.partial(_gate_kernel, block_diag=block_diag, write_v=write_v),
        grid=(r // CHUNK,),
        in_specs=[pl.BlockSpec((CHUNK, D_GMLP), lambda c: (c, 0)),
                  pl.BlockSpec((CHUNK, D_GMLP), lambda c: (c, nb)),
                  pl.BlockSpec((None, 1, D_GMLP), lambda c: (l, 0, 0)),
                  pl.BlockSpec((None, 1, D_GMLP), lambda c: (l, 0, 0)),
                  pl.BlockSpec((GMLP_GROUPS, CHUNK, CHUNK), lambda c: (0, 0, 0)),
                  pl.BlockSpec((CHUNK, GMLP_GROUPS), lambda c: (0, 0))],
        out_specs=out_specs,
        out_shape=out_shape,
        compiler_params=_cparams(1),
        name="spatial_gate",
    )(z, z, ln_g.reshape(-1, 1, D_GMLP), ln_b.reshape(-1, 1, D_GMLP), w_mix, b_t)
    return res if write_v else (res[0], None)


def _down_kernel(*refs, conv_period, n_mod, nk):
    it = iter(refs)
    if conv_period:
        g_ref, v_ref, p2_ref, p1_ref, cw_ref, cb_ref = (next(it) for _ in range(6))
    else:
        a_ref = next(it)
    w_ref, res_ref, gate_ref, pg_ref, pb_ref = (next(it) for _ in range(5))
    mod_refs = [(next(it), next(it)) for _ in range(n_mod)]
    xo_ref = next(it)
    h_refs = [next(it) for _ in range(n_mod)]
    acc = next(it)
    k = pl.program_id(1)

    @pl.when(k == 0)
    def _():
        acc[...] = jnp.zeros_like(acc)

    if conv_period:
        g = g_ref[...].astype(F32)
        tm, tk = g.shape
        t = lax.broadcasted_iota(jnp.int32, (tm, tk), 0) & (conv_period - 1)
        p2, p1 = p2_ref[...], p1_ref[...]
        if p2.shape[0] != tm:
            p2 = jnp.tile(p2, (tm // p2.shape[0], 1))
            p1 = jnp.tile(p1, (tm // p1.shape[0], 1))
        gm2 = jnp.where(t < 2, p2, pltpu.roll(g, 2, 0))
        gm1 = jnp.where(t < 1, p1, pltpu.roll(g, 1, 0))
        gc = cb_ref[...] + cw_ref[0:1, :] * gm2 + cw_ref[1:2, :] * gm1 + cw_ref[2:3, :] * g
        a = (gc * jax.nn.sigmoid(gc) * v_ref[...].astype(F32)).astype(BF)
    else:
        a = a_ref[...]
    acc[...] += jnp.dot(a, w_ref[...].astype(BF), preferred_element_type=F32)

    @pl.when(k == nk - 1)
    def _():
        y = ALPHA * res_ref[...] + gate_ref[...] * acc[...]
        mu = jnp.mean(y, axis=-1, keepdims=True)
        dy = y - mu
        var = jnp.mean(dy * dy, axis=-1, keepdims=True)
        yn = dy * lax.rsqrt(var + EPS) * pg_ref[...] + pb_ref[...]
        xo_ref[...] = yn
        for (sc_ref, sh_ref), h_ref in zip(mod_refs, h_refs):
            h_ref[...] = (yn * (1.0 + sc_ref[...]) + sh_ref[...]).astype(h_ref.dtype)


def mm_down_ln(a, w, l, resid, gate, post_g, post_b, ln_idx, mods, *, conv=None, tm, tk):
    r = resid.shape[0]
    k_dim, n = w.shape[1], w.shape[2]
    nk = k_dim // tk
    row = lambda i, k: i
    in_specs, args = [], []
    if conv is not None:
        gv, p2, p1, cw, cb, period = conv
        pr = p2.shape[0] // (r // tm)
        in_specs += [pl.BlockSpec((tm, tk), lambda i, k: (i, k)),
                     pl.BlockSpec((tm, tk), lambda i, k: (i, k + nk)),
                     pl.BlockSpec((pr, tk), lambda i, k: (i, k)),
                     pl.BlockSpec((pr, tk), lambda i, k: (i, k)),
                     pl.BlockSpec((None, CONV_W, tk), lambda i, k: (l, 0, k)),
                     pl.BlockSpec((None, 1, tk), lambda i, k: (l, 0, k))]
        args += [gv, gv, p2, p1, cw, cb.reshape(cb.shape[0], 1, k_dim)]
        conv_period = tm if period is None else period
    else:
        in_specs.append(pl.BlockSpec((tm, tk), lambda i, k: (i, k)))
        args.append(a)
        conv_period = 0
    in_specs += [pl.BlockSpec((None, tk, n), lambda i, k: (l, k, 0)),
                 pl.BlockSpec((tm, n), lambda i, k: (i, 0)),
                 gate.spec(tm, row),
                 pl.BlockSpec((None, 1, n), lambda i, k: (ln_idx, 0, 0)),
                 pl.BlockSpec((None, 1, n), lambda i, k: (ln_idx, 0, 0))]
    args += [w, resid, gate.arr, post_g.reshape(-1, 1, n), post_b.reshape(-1, 1, n)]
    for sc, sh in mods:
        in_specs += [sc.spec(tm, row), sh.spec(tm, row)]
        args += [sc.arr, sh.arr]
    out_shape = [jax.ShapeDtypeStruct((r, n), F32)] + [jax.ShapeDtypeStruct((r, n), BF)] * len(mods)
    out_specs = [pl.BlockSpec((tm, n), lambda i, k: (i, 0)) for _ in out_shape]
    res = pl.pallas_call(
        functools.partial(_down_kernel, conv_period=conv_period, n_mod=len(mods), nk=nk),
        grid=(r // tm, nk),
        in_specs=in_specs,
        out_specs=out_specs,
        out_shape=out_shape,
        scratch_shapes=[pltpu.VMEM((tm, n), F32)],
        compiler_params=_cparams(2),
        name="mm_down_ln",
    )(*args)
    return res[0], list(res[1:])


def _dkv_kernel(x_ref, w_ref, g_ref, cs_ref, lat_ref, kr_ref):
    a = jnp.dot(x_ref[...], w_ref[...].astype(BF), preferred_element_type=F32)
    lat = a[:, :KV_LORA]
    lat_ref[...] = lat * lax.rsqrt(jnp.mean(lat * lat, axis=-1, keepdims=True) + EPS) * g_ref[...]
    t = a[:, KV_LORA:] * cs_ref[...]
    kr_ref[...] = t[:, :QK_ROPE] + t[:, QK_ROPE:]


def dkv(x, w_ext, norm_g, cs, cs_idx, tm):
    r, k = x.shape
    n = w_ext.shape[1]
    return pl.pallas_call(
        _dkv_kernel,
        grid=(r // tm,),
        in_specs=[pl.BlockSpec((tm, k), lambda i: (i, 0)),
                  pl.BlockSpec((k, n), lambda i: (0, 0)),
                  pl.BlockSpec((1, KV_LORA), lambda i: (0, 0)),
                  pl.BlockSpec((tm, 2 * QK_ROPE), lambda i: (cs_idx(i), 0))],
        out_specs=[pl.BlockSpec((tm, KV_LORA), lambda i: (i, 0)),
                   pl.BlockSpec((tm, QK_ROPE), lambda i: (i, 0))],
        out_shape=[jax.ShapeDtypeStruct((r, KV_LORA), F32), jax.ShapeDtypeStruct((r, QK_ROPE), F32)],
        compiler_params=_cparams(1),
        name="dkv",
    )(x, w_ext, norm_g.reshape(1, KV_LORA), cs)


def _flash_kernel(q_ref, kn_ref, kr_ref, v_ref, o_ref, *, tq):
    qi = pl.program_id(2)
    q = q_ref[...]

    def step(j, carry, masked):
        m, l, acc = carry
        ks = pl.multiple_of(j * tq, tq)
        kk = jnp.concatenate([kn_ref[pl.ds(ks, tq), :], kr_ref[pl.ds(ks, tq), :]], axis=1)
        s = lax.dot_general(q, kk, (((1,), (1,)), ((), ())), preferred_element_type=F32)
        if masked:
            row = lax.broadcasted_iota(jnp.int32, (tq, tq), 0)
            col = lax.broadcasted_iota(jnp.int32, (tq, tq), 1)
            s = jnp.where(row >= col, s, -jnp.inf)
        m_new = jnp.maximum(m, jnp.max(s, axis=-1, keepdims=True))
        alpha = jnp.exp(m - m_new)
        p = jnp.exp(s - m_new)
        l = alpha * l + jnp.sum(p, axis=-1, keepdims=True)
        acc = alpha * acc + jnp.dot(p.astype(BF), v_ref[pl.ds(ks, tq), :], preferred_element_type=F32)
        return m_new, l, acc

    init = (jnp.full((tq, 1), -jnp.inf, F32), jnp.zeros((tq, 1), F32), jnp.zeros((tq, V_HEAD), F32))
    carry = lax.fori_loop(0, qi, lambda j, c: step(j, c, False), init)
    _, l, acc = step(qi, carry, True)
    o_ref[...] = (acc / l).astype(o_ref.dtype)


def flash_prompt(q, kn, kr2, v, bsz, seq, tq):
    r = q.shape[0]
    nq = seq // tq
    return pl.pallas_call(
        functools.partial(_flash_kernel, tq=tq),
        grid=(bsz, N_HEADS, nq),
        in_specs=[pl.BlockSpec((tq, HEAD_W), lambda b, h, i: (b * nq + i, h)),
                  pl.BlockSpec((seq, QK_NOPE), lambda b, h, i: (b, h)),
                  pl.BlockSpec((seq, 2 * QK_ROPE), lambda b, h, i: (b, 0)),
                  pl.BlockSpec((seq, V_HEAD), lambda b, h, i: (b, h))],
        out_specs=pl.BlockSpec((tq, V_HEAD), lambda b, h, i: (b * nq + i, h)),
        out_shape=jax.ShapeDtypeStruct((r, N_HEADS * V_HEAD), BF),
        compiler_params=_cparams(3),
        name="flash_prompt",
    )(q, kn, kr2, v)


def _qlat_kernel(x_ref, w_ref, o_ref):
    x = x_ref[...]
    o_ref[:, :KV_LORA] = jnp.dot(x[:, :QK_NOPE], w_ref[...].astype(BF), preferred_element_type=F32)
    o_ref[:, KV_LORA:] = x[:, QK_NOPE:].astype(F32)


def q_to_latent(q, w_uk_t):
    r = q.shape[0]
    return pl.pallas_call(
        _qlat_kernel,
        grid=(N_HEADS,),
        in_specs=[pl.BlockSpec((r, HEAD_W), lambda h: (0, h)),
                  pl.BlockSpec((None, QK_NOPE, KV_LORA), lambda h: (h, 0, 0))],
        out_specs=pl.BlockSpec((None, r, KV_LORA + 2 * QK_ROPE), lambda h: (h, 0, 0)),
        out_shape=jax.ShapeDtypeStruct((N_HEADS, r, KV_LORA + 2 * QK_ROPE), F32),
        compiler_params=_cparams(1),
        name="q_to_latent",
    )(q, w_uk_t)


def _paged_kernel(pt_ref, q_ref, *refs, n_pages, n_steps, t_new):
    lat_refs = refs[:n_pages]
    kr_refs = refs[n_pages:2 * n_pages]
    latn_ref, krn_ref, o_ref, ql_s, qr_s, m_s, l_s, acc_s = refs[2 * n_pages:]
    jg = pl.program_id(1)
    rows = N_HEADS * t_new

    @pl.when(jg == 0)
    def _():
        q = q_ref[...].reshape(rows, KV_LORA + 2 * QK_ROPE)
        ql_s[...] = q[:, :KV_LORA].astype(BF)
        qr_s[...] = (q[:, KV_LORA:KV_LORA + QK_ROPE] + q[:, KV_LORA + QK_ROPE:]).astype(BF)
        m_s[...] = jnp.full(m_s.shape, -jnp.inf, F32)
        l_s[...] = jnp.zeros(l_s.shape, F32)
        acc_s[...] = jnp.zeros(acc_s.shape, F32)

    contract_last = (((1,), (1,)), ((), ()))

    def update(lat, kr, mask):
        s = (lax.dot_general(ql_s[...], lat, contract_last, preferred_element_type=F32)
             + lax.dot_general(qr_s[...], kr, contract_last, preferred_element_type=F32))
        if mask is not None:
            s = jnp.where(mask, s, -jnp.inf)
        m_old = m_s[...]
        m_new = jnp.maximum(m_old, jnp.max(s, axis=-1, keepdims=True))
        alpha = jnp.exp(m_old - m_new)
        p = jnp.exp(s - m_new)
        l_s[...] = alpha * l_s[...] + jnp.sum(p, axis=-1, keepdims=True)
        acc_s[...] = alpha * acc_s[...] + jnp.dot(p.astype(BF), lat, preferred_element_type=F32)
        m_s[...] = m_new

    lat = jnp.concatenate([r[...].astype(BF) for r in lat_refs], axis=0)
    kr = jnp.concatenate([r[...].astype(BF) for r in kr_refs], axis=0)
    update(lat, kr, None)

    @pl.when(jg == n_steps - 1)
    def _():
        n_pad = latn_ref.shape[0]
        tq = lax.broadcasted_iota(jnp.int32, (rows, n_pad), 0) & (t_new - 1)
        tk = lax.broadcasted_iota(jnp.int32, (rows, n_pad), 1)
        update(latn_ref[...], krn_ref[...], tk <= tq)
        o_ref[...] = (acc_s[...] / l_s[...]).reshape(N_HEADS, t_new, KV_LORA)


def paged_attention(q_lat, cache_lat, cache_kr, page_table, lat_new, kr_new, t_new):
    bsz, n_tbl = page_table.shape
    page = cache_lat.shape[1]
    npg = PAGES_PER_STEP
    n_steps = n_tbl // npg
    rows = N_HEADS * t_new
    n_pad = lat_new.shape[1]
    qw = KV_LORA + 2 * QK_ROPE

    def page_spec(width, p):
        return pl.BlockSpec((None, page, width), lambda b, j, pt: (pt[b * n_tbl + j * npg + p], 0, 0))

    in_specs = [pl.BlockSpec((N_HEADS, t_new, qw), lambda b, j, pt: (0, b, 0))]
    in_specs += [page_spec(KV_LORA, p) for p in range(npg)]
    in_specs += [page_spec(QK_ROPE, p) for p in range(npg)]
    in_specs += [pl.BlockSpec((None, n_pad, KV_LORA), lambda b, j, pt: (b, 0, 0)),
                 pl.BlockSpec((None, n_pad, QK_ROPE), lambda b, j, pt: (b, 0, 0))]
    grid_spec = pltpu.PrefetchScalarGridSpec(
        num_scalar_prefetch=1,
        grid=(bsz, n_steps),
        in_specs=in_specs,
        out_specs=pl.BlockSpec((N_HEADS, t_new, KV_LORA), lambda b, j, pt: (0, b, 0)),
        scratch_shapes=[pltpu.VMEM((rows, KV_LORA), BF), pltpu.VMEM((rows, QK_ROPE), BF),
                        pltpu.VMEM((rows, 1), F32), pltpu.VMEM((rows, 1), F32),
                        pltpu.VMEM((rows, KV_LORA), F32)])
    return pl.pallas_call(
        functools.partial(_paged_kernel, n_pages=npg, n_steps=n_steps, t_new=t_new),
        grid_spec=grid_spec,
        out_shape=jax.ShapeDtypeStruct((N_HEADS, bsz * t_new, KV_LORA), F32),
        compiler_params=_cparams(2),
        name="paged_attention",
    )(page_table.reshape(-1), q_lat, *([cache_lat] * npg), *([cache_kr] * npg), lat_new, kr_new)


def _ov_kernel(o_ref, w_ref, out_ref):
    out_ref[...] = jnp.dot(o_ref[...].astype(BF), w_ref[...].astype(BF),
                           preferred_element_type=F32).astype(out_ref.dtype)


def latent_to_v(o_lat, w_uv2d):
    r = o_lat.shape[1]
    return pl.pallas_call(
        _ov_kernel,
        grid=(N_HEADS,),
        in_specs=[pl.BlockSpec((None, r, KV_LORA), lambda h: (h, 0, 0)),
                  pl.BlockSpec((KV_LORA, V_HEAD), lambda h: (0, h))],
        out_specs=pl.BlockSpec((r, V_HEAD), lambda h: (0, h)),
        out_shape=jax.ShapeDtypeStruct((r, N_HEADS * V_HEAD), BF),
        compiler_params=_cparams(1),
        name="latent_to_v",
    )(o_lat, w_uv2d)


def _rope_swap(w):
    half = QK_ROPE // 2
    return jnp.concatenate([-w[..., half:], w[..., :half]], axis=-1)


def _rope_tables(pos):
    half = QK_ROPE // 2
    inv = jnp.power(ROPE_THETA, -jnp.arange(half, dtype=F32) / half)
    ang = pos.astype(F32)[:, None] * inv[None, :]
    cos, sin = jnp.cos(ang), jnp.sin(ang)
    cs = jnp.concatenate([cos, cos, sin, sin], axis=-1)
    q_tab = SM_SCALE * jnp.concatenate([jnp.ones((pos.shape[0], QK_NOPE), F32), cs], axis=-1)
    return cs, q_tab


def _trunk(x3, c_mods, kv_mod, pos, conv_state, paged, W, tm):
    bsz, t_rows, d = x3.shape
    r = bsz * t_rows
    x = x3.reshape(r, d)
    is_prompt = paged is None
    tiles_per_seq = max(t_rows // tm, 1)
    cs, q_tab = _rope_tables(pos)
    if is_prompt:
        tab_idx = lambda i: i % tiles_per_seq
    else:
        cs, q_tab = jnp.tile(cs, (tm // t_rows, 1)), jnp.tile(q_tab, (tm // t_rows, 1))
        tab_idx = lambda i: 0

    def vecs(mat, n):
        return [RowVec(mat[:, i * d:(i + 1) * d], t_rows) for i in range(n)]

    layer_mods = [vecs(m, 6) for m in c_mods]
    sc_kv_sh = vecs(kv_mod, 2)

    conv_new, v_rows = [], []
    latent = k_rope = None
    kv_ctx = None
    tm_d = 512 if is_prompt else 256
    h = modulate(x, layer_mods[0][1], layer_mods[0][0], tm_d)
    for l in range(DEPTH):
        sh1, sc1, g1, sh2, sc2, g2 = layer_mods[l]
        if l < N_A_LAYERS:
            z = mm(h, W["gm_w_in"], l, bias=W["gm_b_in"], act="gelu", tm=tm, tn=1024)
            if is_prompt:
                w_mix, b_t, bd = W["gm_w_s"][l], W["gm_b_s"][l].T, 0
            else:
                reps = CHUNK // t_rows
                w_mix = jnp.tile(W["gm_w_s"][l][:, :t_rows, :t_rows], (1, reps, reps))
                b_t = jnp.tile(W["gm_b_s"][l].T[:t_rows], (reps, 1))
                bd = t_rows
            gated, vn = spatial_gate(z, W["gm_ln_g"], W["gm_ln_b"], l, w_mix, b_t,
                                     block_diag=bd, write_v=not is_prompt)
            if vn is not None:
                v_rows.append(vn.reshape(bsz, t_rows, D_GMLP))
            x, (h,) = mm_down_ln(gated, W["gm_w_out"], l, x, g1, W["post_ln_g"], W["post_ln_b"],
                                 2 * l, [(sc2, sh2)], tm=tm_d, tk=512)
        else:
            j = l - N_A_LAYERS
            cq = mm(h, W["w_dq"], j, rms_g=W["q_norm_g"], tm=tm, tn=Q_LORA)
            q = mm(cq, W["w_uq_ext"], j, tab=q_tab, tab_idx=tab_idx, tm=tm, tn=1024, sub=HEAD_W)
            if is_prompt:
                kn, kr2, v = kv_ctx
                o = flash_prompt(q, kn, kr2, v, bsz, t_rows, 512)
            else:
                q_lat = q_to_latent(q, W["w_uk_t"])
                lat_new, kr_new = kv_ctx
                o_lat = paged_attention(q_lat, paged[0], paged[1], paged[2], lat_new, kr_new, t_rows)
                o = latent_to_v(o_lat, W["w_uv2d"])
            x, (h,) = mm_down_ln(o, W["w_o2d"], j, x, g1, W["post_ln_g"], W["post_ln_b"],
                                 2 * l, [(sc2, sh2)], tm=tm_d, tk=512)
        gv = mm(h, W["ffn_w_up"], l, tm=tm, tn=1024)
        g3 = gv[:, :D_FF].reshape(bsz, t_rows, D_FF)
        conv_new.append(g3[:, t_rows - (CONV_W - 1):, :].astype(F32))
        if is_prompt:
            n_tiles = r // tm_d
            last2 = gv[:, :D_FF].reshape(n_tiles, tm_d, D_FF)[:, tm_d - 2:, :].astype(F32)
            prev2 = jnp.concatenate([jnp.zeros((1, 2, D_FF), F32), last2[:-1]], axis=0)
            starts = (jnp.arange(n_tiles) * tm_d) % t_rows == 0
            prev2 = jnp.where(starts[:, None, None], 0.0, prev2)
            period = None
        else:
            prev2 = conv_state[l].astype(F32)
            period = t_rows
        p2 = jnp.pad(prev2, ((0, 0), (0, 6), (0, 0))).reshape(-1, D_FF)
        p1 = jnp.pad(prev2[:, 1:2], ((0, 0), (0, 7), (0, 0))).reshape(-1, D_FF)
        mods = []
        if l + 1 < DEPTH:
            nsh1, nsc1 = layer_mods[l + 1][0], layer_mods[l + 1][1]
            mods.append((nsc1, nsh1))
        if l == N_A_LAYERS - 1:
            mods.append((sc_kv_sh[1], sc_kv_sh[0]))
        x, hs = mm_down_ln(None, W["ffn_w_down"], l, x, g2, W["post_ln_g"], W["post_ln_b"], 2 * l + 1, mods,
                           conv=(gv, p2, p1, W["ffn_conv_w"], W["ffn_conv_b"], period), tm=tm_d, tk=512)
        if hs:
            h = hs[0]
        if l == N_A_LAYERS - 1:
            latent, k_rope = dkv(hs[1], W["w_dkv_ext"], W["kv_norm_g"], cs, tab_idx, tm)
            lat_bf = latent.astype(BF)
            if is_prompt:
                kn = mm(lat_bf, W["w_uk2d"][None], 0, tm=tm, tn=1024)
                v = mm(lat_bf, W["w_uv2d"][None], 0, tm=tm, tn=1024)
                kr2 = jnp.concatenate([k_rope, k_rope], axis=-1).astype(BF)
                kv_ctx = (kn, kr2, v)
            else:
                pad = ((0, 0), (0, 16 - t_rows), (0, 0))
                kv_ctx = (jnp.pad(lat_bf.reshape(bsz, t_rows, KV_LORA), pad),
                          jnp.pad(k_rope.astype(BF).reshape(bsz, t_rows, QK_ROPE), pad))
    y = x.reshape(bsz, t_rows, d)
    return (y, latent.reshape(bsz, t_rows, KV_LORA), k_rope.reshape(bsz, t_rows, QK_ROPE),
            jnp.stack(conv_new), jnp.stack(v_rows) if v_rows else None)


def kernel(x_prompt, x_sample, cache_kv_latent, cache_k_rope, state_conv, page_table, c_prompt, c_sample,
           ada_w, ada_b, post_ln_g, post_ln_b,
           gm_w_in, gm_b_in, gm_ln_g, gm_ln_b, gm_w_s, gm_b_s, gm_w_out,
           kv_ada_w, kv_ada_b, w_dkv, kv_norm_g, w_uk, w_uv,
           w_dq, q_norm_g, w_uq, w_o,
           ffn_w_up, ffn_conv_w, ffn_conv_b, ffn_w_down):
    d = D_MODEL
    n_b = w_dq.shape[0]
    w_uq_rope = w_uq[..., QK_NOPE:]
    w_uq_ext = jnp.concatenate([w_uq[..., :QK_NOPE], w_uq_rope, _rope_swap(w_uq_rope)], axis=-1)
    w_dkv_rope = w_dkv[:, KV_LORA:]
    W = dict(
        gm_w_in=gm_w_in, gm_b_in=gm_b_in, gm_ln_g=gm_ln_g, gm_ln_b=gm_ln_b, gm_w_s=gm_w_s, gm_b_s=gm_b_s,
        gm_w_out=gm_w_out, post_ln_g=post_ln_g.reshape(2 * DEPTH, d), post_ln_b=post_ln_b.reshape(2 * DEPTH, d),
        w_dq=w_dq, q_norm_g=q_norm_g,
        w_uq_ext=w_uq_ext.reshape(n_b, Q_LORA, N_HEADS * HEAD_W),
        w_o2d=w_o.reshape(n_b, N_HEADS * V_HEAD, d),
        w_dkv_ext=jnp.concatenate([w_dkv, _rope_swap(w_dkv_rope)], axis=-1),
        kv_norm_g=kv_norm_g,
        w_uk2d=w_uk.reshape(KV_LORA, N_HEADS * QK_NOPE),
        w_uv2d=w_uv.reshape(KV_LORA, N_HEADS * V_HEAD),
        w_uk_t=jnp.transpose(w_uk, (1, 2, 0)),
        ffn_w_up=ffn_w_up, ffn_conv_w=ffn_conv_w, ffn_conv_b=ffn_conv_b, ffn_w_down=ffn_w_down,
    )

    n_p, n_s = c_prompt.shape[0], c_sample.shape[0]
    n_c = -(-(n_p + n_s) // 8) * 8
    c_all = jnp.pad(jnp.concatenate([c_prompt, c_sample], axis=0), ((0, n_c - n_p - n_s), (0, 0)))
    mods = [mm(c_all, ada_w, l, bias=ada_b, x_act="silu", out_dtype=F32, tm=n_c, tn=1024) for l in range(DEPTH)]
    kv_mod = mm(c_all, kv_ada_w[None], 0, bias=kv_ada_b[None], x_act="silu", out_dtype=F32, tm=n_c, tn=1024)

    seq_p = x_prompt.shape[1]
    y_p, lat_p, kr_p, conv_p, _ = _trunk(
        x_prompt, [m[:n_p] for m in mods], kv_mod[:n_p], jnp.arange(seq_p, dtype=jnp.int32),
        None, None, W, tm=1024)

    past = page_table.shape[1] * cache_kv_latent.shape[1]
    seq_s = x_sample.shape[1]
    y_s, lat_s, kr_s, conv_s, v_s = _trunk(
        x_sample, [m[n_p:n_p + n_s] for m in mods], kv_mod[n_p:n_p + n_s],
        past + jnp.arange(seq_s, dtype=jnp.int32), state_conv,
        (cache_kv_latent, cache_k_rope, page_table), W, tm=1024)

    return (y_p, y_s, lat_p, kr_p, lat_s, kr_s, conv_p, conv_s, v_s)
```

```python
import functools

import numpy as np
import jax
import jax.numpy as jnp
from jax import lax
from jax.experimental import pallas as pl
from jax.experimental.pallas import tpu as pltpu

F32 = jnp.float32
BF = jnp.bfloat16

D_MODEL = 2048
DEPTH = 4
N_A_LAYERS = DEPTH // 2
CHUNK = 128
GMLP_GROUPS = 16
D_GMLP = 3 * D_MODEL
GROUP_DIM = D_GMLP // GMLP_GROUPS
N_HEADS = 16
QK_NOPE = 128
QK_ROPE = 64
V_HEAD = 128
KV_LORA = 512
Q_LORA = 512
ROPE_THETA = 10000.0
SM_SCALE = (QK_NOPE + QK_ROPE) ** -0.5
D_FF = 5632
CONV_W = 3
ALPHA = (2 * DEPTH) ** 0.25
EPS = 1e-5
HEAD_W = 2 * QK_NOPE
PAGES_PER_STEP = 32
LN_ROWS = 256
LOG2E = 1.4426950408889634
VMEM_LIMIT = 56 * 1024 * 1024


def _cparams(n_axes):
    return pltpu.CompilerParams(dimension_semantics=("arbitrary",) * n_axes,
                                vmem_limit_bytes=VMEM_LIMIT)


class RowVec:
    def __init__(self, v, t_rows):
        self.b, self.d = v.shape
        self.t = t_rows
        self.expanded = t_rows < CHUNK
        self.arr = jnp.repeat(v, t_rows, axis=0) if self.expanded else v.reshape(self.b, 1, self.d)

    def spec(self, tm, row_tile):
        if self.expanded:
            return pl.BlockSpec((tm, self.d), lambda *g: (row_tile(*g), 0))
        t = self.t
        return pl.BlockSpec((None, 1, self.d), lambda *g: ((row_tile(*g) * tm) // t, 0, 0))


def _modulate_kernel(x_ref, sc_ref, sh_ref, o_ref):
    o_ref[...] = (x_ref[...] * (1.0 + sc_ref[...]) + sh_ref[...]).astype(o_ref.dtype)


def modulate(x, sc, sh, tm):
    r, d = x.shape
    row = lambda i: i
    return pl.pallas_call(
        _modulate_kernel,
        grid=(r // tm,),
        in_specs=[pl.BlockSpec((tm, d), lambda i: (i, 0)), sc.spec(tm, row), sh.spec(tm, row)],
        out_specs=pl.BlockSpec((tm, d), lambda i: (i, 0)),
        out_shape=jax.ShapeDtypeStruct((r, d), BF),
        compiler_params=_cparams(1),
        name="modulate",
    )(x, sc.arr, sh.arr)


def _mm_kernel(*refs, x_act, has_bias, act, has_rms, has_tab, sub):
    it = iter(refs)
    x_ref, w_ref = next(it), next(it)
    b_ref = next(it) if has_bias else None
    g_ref = next(it) if has_rms else None
    t_ref = next(it) if has_tab else None
    o_ref, wb = next(it), next(it)

    @pl.when(pl.program_id(1) == 0)
    def _():
        wb[...] = w_ref[...].astype(BF)

    xv = x_ref[...]
    if x_act == "silu":
        xf = xv.astype(F32)
        xv = xf * jax.nn.sigmoid(xf)
    xb = xv.astype(BF)
    tn = o_ref.shape[1]
    for c in range(tn // sub):
        sl = slice(c * sub, (c + 1) * sub)
        acc = jnp.dot(xb, wb[:, sl], preferred_element_type=F32)
        if has_bias:
            acc = acc + b_ref[:, sl]
        if act == "gelu":
            acc = 0.5 * acc * (1.0 + lax.erf(acc * np.float32(2.0 ** -0.5)))
        if has_rms:
            acc = acc * lax.rsqrt(jnp.mean(acc * acc, axis=-1, keepdims=True) + EPS) * g_ref[...]
        if has_tab:
            acc = acc * t_ref[...]
        o_ref[:, sl] = acc.astype(o_ref.dtype)


def mm(x, w, l, *, bias=None, x_act=None, act=None, rms_g=None, tab=None, tab_idx=None,
       out_dtype=BF, tm, tn, sub=256):
    m, k = x.shape
    n = w.shape[2]
    sub = min(sub, tn)
    if rms_g is not None:
        assert tn == n
        sub = tn
    in_specs = [pl.BlockSpec((tm, k), lambda j, i: (i, 0)),
                pl.BlockSpec((None, k, tn), lambda j, i: (l, 0, j))]
    args = [x, w]
    for v in (bias, rms_g):
        if v is not None:
            in_specs.append(pl.BlockSpec((None, 1, tn), lambda j, i: (l, 0, j)))
            args.append(v.reshape(v.shape[0], 1, n))
    if tab is not None:
        assert tab.shape[1] == sub
        in_specs.append(pl.BlockSpec((tm, sub), lambda j, i: (tab_idx(i), 0)))
        args.append(tab)
    kern = functools.partial(_mm_kernel, x_act=x_act, has_bias=bias is not None, act=act,
                             has_rms=rms_g is not None, has_tab=tab is not None, sub=sub)
    return pl.pallas_call(
        kern,
        grid=(n // tn, m // tm),
        in_specs=in_specs,
        out_specs=pl.BlockSpec((tm, tn), lambda j, i: (i, j)),
        out_shape=jax.ShapeDtypeStruct((m, n), out_dtype),
        scratch_shapes=[pltpu.VMEM((k, tn), BF)],
        compiler_params=_cparams(2),
        name="mm",
    )(*args)


def _gate_kernel(u_ref, v_ref, lg_ref, lb_ref, w_ref, bt_ref, *out_refs, block_diag, write_v):
    o_ref = out_refs[0]
    v = v_ref[...].astype(F32)
    mu = jnp.mean(v, axis=-1, keepdims=True)
    dv = v - mu
    var = jnp.mean(dv * dv, axis=-1, keepdims=True)
    vn = dv * lax.rsqrt(var + EPS) * lg_ref[...] + lb_ref[...]
    if write_v:
        out_refs[1][...] = vn
    vb = vn.astype(BF)
    row = lax.broadcasted_iota(jnp.int32, (CHUNK, CHUNK), 0)
    col = lax.broadcasted_iota(jnp.int32, (CHUNK, CHUNK), 1)
    mask = row >= col
    if block_diag:
        mask = jnp.logical_and(mask, (row // block_diag) == (col // block_diag))
    for g in range(GMLP_GROUPS):
        sl = slice(g * GROUP_DIM, (g + 1) * GROUP_DIM)
        wm = jnp.where(mask, w_ref[g], 0.0).astype(BF)
        s = jnp.dot(wm, vb[:, sl], preferred_element_type=F32) + bt_ref[:, g:g + 1]
        o_ref[:, sl] = (u_ref[:, sl].astype(F32) * s).astype(o_ref.dtype)


def spatial_gate(z, ln_g, ln_b, l, w_mix, b_t, *, block_diag, write_v):
    r = z.shape[0]
    nb = D_GMLP // D_GMLP
    out_shape = [jax.ShapeDtypeStruct((r, D_GMLP), BF)]
    out_specs = [pl.BlockSpec((CHUNK, D_GMLP), lambda c: (c, 0))]
    if write_v:
        out_shape.append(jax.ShapeDtypeStruct((r, D_GMLP), F32))
        out_specs.append(pl.BlockSpec((CHUNK, D_GMLP), lambda c: (c, 0)))
    res = pl.pallas_call(
        functools.partial(_gate_kernel, block_diag=block_diag, write_v=write_v),
        grid=(r // CHUNK,),
        in_specs=[pl.BlockSpec((CHUNK, D_GMLP), lambda c: (c, 0)),
                  pl.BlockSpec((CHUNK, D_GMLP), lambda c: (c, nb)),
                  pl.BlockSpec((None, 1, D_GMLP), lambda c: (l, 0, 0)),
                  pl.BlockSpec((None, 1, D_GMLP), lambda c: (l, 0, 0)),
                  pl.BlockSpec((GMLP_GROUPS, CHUNK, CHUNK), lambda c: (0, 0, 0)),
                  pl.BlockSpec((CHUNK, GMLP_GROUPS), lambda c: (0, 0))],
        out_specs=out_specs,
        out_shape=out_shape,
        compiler_params=_cparams(1),
        name="spatial_gate",
    )(z, z, ln_g.reshape(-1, 1, D_GMLP), ln_b.reshape(-1, 1, D_GMLP), w_mix, b_t)
    return res if write_v else (res[0], None)


def _down_kernel(*refs, conv_period, n_mod, nk):
    it = iter(refs)
    if conv_period:
        g_ref, v_ref, p2_ref, p1_ref, cw_ref, cb_ref = (next(it) for _ in range(6))
    else:
        a_ref = next(it)
    w_ref, res_ref, gate_ref, pg_ref, pb_ref = (next(it) for _ in range(5))
    mod_refs = [(next(it), next(it)) for _ in range(n_mod)]
    xo_ref = next(it)
    h_refs = [next(it) for _ in range(n_mod)]
    k = pl.program_id(1)
    tm = xo_ref.shape[0]

    @pl.when(k == 0)
    def _():
        xo_ref[...] = jnp.zeros_like(xo_ref)

    if conv_period:
        g = g_ref[...].astype(F32)
        tk = g.shape[1]
        t = lax.broadcasted_iota(jnp.int32, (tm, tk), 0) & (conv_period - 1)
        p2, p1 = p2_ref[...], p1_ref[...]
        if p2.shape[0] != tm:
            p2 = jnp.tile(p2, (tm // p2.shape[0], 1))
            p1 = jnp.tile(p1, (tm // p1.shape[0], 1))
        gm2 = jnp.where(t < 2, p2, pltpu.roll(g, 2, 0))
        gm1 = jnp.where(t < 1, p1, pltpu.roll(g, 1, 0))
        gc = cb_ref[...] + cw_ref[0:1, :] * gm2 + cw_ref[1:2, :] * gm1 + cw_ref[2:3, :] * g
        a = (gc * jax.nn.sigmoid(gc) * v_ref[...].astype(F32)).astype(BF)
    else:
        a = a_ref[...]
    xo_ref[...] += jnp.dot(a, w_ref[...], preferred_element_type=F32)

    @pl.when(k == nk - 1)
    def _():
        rc = min(tm, LN_ROWS)

        def rows(ref, r0):
            return ref[pl.ds(r0, rc), :] if ref.shape[0] == tm else ref[...]

        def chunk(c, carry):
            r0 = pl.multiple_of(c * rc, rc)
            y = ALPHA * res_ref[pl.ds(r0, rc), :] + rows(gate_ref, r0) * xo_ref[pl.ds(r0, rc), :]
            mu = jnp.mean(y, axis=-1, keepdims=True)
            dy = y - mu
            var = jnp.mean(dy * dy, axis=-1, keepdims=True)
            yn = dy * lax.rsqrt(var + EPS) * pg_ref[...] + pb_ref[...]
            xo_ref[pl.ds(r0, rc), :] = yn
            for (sc_ref, sh_ref), h_ref in zip(mod_refs, h_refs):
                h_ref[pl.ds(r0, rc), :] = (yn * (1.0 + rows(sc_ref, r0)) + rows(sh_ref, r0)).astype(h_ref.dtype)
            return carry

        lax.fori_loop(0, tm // rc, chunk, 0)


def mm_down_ln(a, w, l, resid, gate, post_g, post_b, ln_idx, mods, *, conv=None, tm, tk):
    r = resid.shape[0]
    k_dim, n = w.shape[1], w.shape[2]
    nk = k_dim // tk
    row = lambda i, k: i
    in_specs, args = [], []
    if conv is not None:
        gv, p2, p1, cw, cb, period = conv
        pr = p2.shape[0] // (r // tm)
        in_specs += [pl.BlockSpec((tm, tk), lambda i, k: (i, k)),
                     pl.BlockSpec((tm, tk), lambda i, k: (i, k + nk)),
                     pl.BlockSpec((pr, tk), lambda i, k: (i, k)),
                     pl.BlockSpec((pr, tk), lambda i, k: (i, k)),
                     pl.BlockSpec((None, CONV_W, tk), lambda i, k: (l, 0, k)),
                     pl.BlockSpec((None, 1, tk), lambda i, k: (l, 0, k))]
        args += [gv, gv, p2, p1, cw, cb.reshape(cb.shape[0], 1, k_dim)]
        conv_period = tm if period is None else period
    else:
        in_specs.append(pl.BlockSpec((tm, tk), lambda i, k: (i, k)))
        args.append(a)
        conv_period = 0
    in_specs += [pl.BlockSpec((None, tk, n), lambda i, k: (l, k, 0)),
                 pl.BlockSpec((tm, n), lambda i, k: (i, 0), pipeline_mode=pl.Buffered(1)),
                 gate.spec(tm, row),
                 pl.BlockSpec((None, 1, n), lambda i, k: (ln_idx, 0, 0)),
                 pl.BlockSpec((None, 1, n), lambda i, k: (ln_idx, 0, 0))]
    args += [w, resid, gate.arr, post_g.reshape(-1, 1, n), post_b.reshape(-1, 1, n)]
    for sc, sh in mods:
        in_specs += [sc.spec(tm, row), sh.spec(tm, row)]
        args += [sc.arr, sh.arr]
    out_shape = [jax.ShapeDtypeStruct((r, n), F32)] + [jax.ShapeDtypeStruct((r, n), BF)] * len(mods)
    out_specs = [pl.BlockSpec((tm, n), lambda i, k: (i, 0)) for _ in out_shape]
    res = pl.pallas_call(
        functools.partial(_down_kernel, conv_period=conv_period, n_mod=len(mods), nk=nk),
        grid=(r // tm, nk),
        in_specs=in_specs,
        out_specs=out_specs,
        out_shape=out_shape,
        compiler_params=_cparams(2),
        name="mm_down_ln",
    )(*args)
    return res[0], list(res[1:])


def _dkv_kernel(x_ref, w_ref, g_ref, cs_ref, lat_ref, kr_ref):
    a = jnp.dot(x_ref[...], w_ref[...].astype(BF), preferred_element_type=F32)
    lat = a[:, :KV_LORA]
    lat_ref[...] = lat * lax.rsqrt(jnp.mean(lat * lat, axis=-1, keepdims=True) + EPS) * g_ref[...]
    t = a[:, KV_LORA:] * cs_ref[...]
    kr_ref[...] = t[:, :QK_ROPE] + t[:, QK_ROPE:]


def dkv(x, w_ext, norm_g, cs, cs_idx, tm):
    r, k = x.shape
    n = w_ext.shape[1]
    return pl.pallas_call(
        _dkv_kernel,
        grid=(r // tm,),
        in_specs=[pl.BlockSpec((tm, k), lambda i: (i, 0)),
                  pl.BlockSpec((k, n), lambda i: (0, 0)),
                  pl.BlockSpec((1, KV_LORA), lambda i: (0, 0)),
                  pl.BlockSpec((tm, 2 * QK_ROPE), lambda i: (cs_idx(i), 0))],
        out_specs=[pl.BlockSpec((tm, KV_LORA), lambda i: (i, 0)),
                   pl.BlockSpec((tm, QK_ROPE), lambda i: (i, 0))],
        out_shape=[jax.ShapeDtypeStruct((r, KV_LORA), F32), jax.ShapeDtypeStruct((r, QK_ROPE), F32)],
        compiler_params=_cparams(1),
        name="dkv",
    )(x, w_ext, norm_g.reshape(1, KV_LORA), cs)


def _flash_kernel(q_ref, kn_ref, kr_ref, v_ref, o_ref, *, tq):
    qi = pl.program_id(2)
    q = q_ref[...]

    def step(j, carry, masked):
        m, l, acc = carry
        ks = pl.multiple_of(j * tq, tq)
        kk = jnp.concatenate([kn_ref[pl.ds(ks, tq), :], kr_ref[pl.ds(ks, tq), :]], axis=1)
        s = lax.dot_general(q, kk, (((1,), (1,)), ((), ())), preferred_element_type=F32)
        if masked:
            row = lax.broadcasted_iota(jnp.int32, (tq, tq), 0)
            col = lax.broadcasted_iota(jnp.int32, (tq, tq), 1)
            s = jnp.where(row >= col, s, -jnp.inf)
        m_new = jnp.maximum(m, jnp.max(s, axis=-1, keepdims=True))
        alpha = jnp.exp2(m - m_new)
        p = jnp.exp2(s - m_new)
        l = alpha * l + jnp.sum(p, axis=-1, keepdims=True)
        acc = alpha * acc + jnp.dot(p.astype(BF), v_ref[pl.ds(ks, tq), :], preferred_element_type=F32)
        return m_new, l, acc

    init = (jnp.full((tq, 1), -jnp.inf, F32), jnp.zeros((tq, 1), F32), jnp.zeros((tq, V_HEAD), F32))
    carry = lax.fori_loop(0, qi, lambda j, c: step(j, c, False), init)
    _, l, acc = step(qi, carry, True)
    o_ref[...] = (acc / l).astype(o_ref.dtype)


def flash_prompt(q, kn, kr2, v, bsz, seq, tq):
    r = q.shape[0]
    nq = seq // tq
    return pl.pallas_call(
        functools.partial(_flash_kernel, tq=tq),
        grid=(bsz, N_HEADS, nq),
        in_specs=[pl.BlockSpec((tq, HEAD_W), lambda b, h, i: (b * nq + i, h)),
                  pl.BlockSpec((seq, QK_NOPE), lambda b, h, i: (b, h)),
                  pl.BlockSpec((seq, 2 * QK_ROPE), lambda b, h, i: (b, 0)),
                  pl.BlockSpec((seq, V_HEAD), lambda b, h, i: (b, h))],
        out_specs=pl.BlockSpec((tq, V_HEAD), lambda b, h, i: (b * nq + i, h)),
        out_shape=jax.ShapeDtypeStruct((r, N_HEADS * V_HEAD), BF),
        compiler_params=_cparams(3),
        name="flash_prompt",
    )(q, kn, kr2, v)


def _qlat_kernel(x_ref, w_ref, o_ref):
    x = x_ref[...]
    o_ref[:, :KV_LORA] = jnp.dot(x[:, :QK_NOPE], w_ref[...].astype(BF), preferred_element_type=F32)
    o_ref[:, KV_LORA:] = x[:, QK_NOPE:].astype(F32)


def q_to_latent(q, w_uk_t):
    r = q.shape[0]
    return pl.pallas_call(
        _qlat_kernel,
        grid=(N_HEADS,),
        in_specs=[pl.BlockSpec((r, HEAD_W), lambda h: (0, h)),
                  pl.BlockSpec((None, QK_NOPE, KV_LORA), lambda h: (h, 0, 0))],
        out_specs=pl.BlockSpec((None, r, KV_LORA + 2 * QK_ROPE), lambda h: (h, 0, 0)),
        out_shape=jax.ShapeDtypeStruct((N_HEADS, r, KV_LORA + 2 * QK_ROPE), F32),
        compiler_params=_cparams(1),
        name="q_to_latent",
    )(q, w_uk_t)


def _paged_kernel(pt_ref, q_ref, *refs, n_pages, n_steps, t_new):
    lat_refs = refs[:n_pages]
    kr_refs = refs[n_pages:2 * n_pages]
    latn_ref, krn_ref, o_ref, ql_s, qr_s, m_s, l_s, acc_s = refs[2 * n_pages:]
    jg = pl.program_id(1)
    rows = N_HEADS * t_new

    @pl.when(jg == 0)
    def _():
        q = q_ref[...].reshape(rows, KV_LORA + 2 * QK_ROPE)
        ql_s[...] = q[:, :KV_LORA].astype(BF)
        qr_s[...] = (q[:, KV_LORA:KV_LORA + QK_ROPE] + q[:, KV_LORA + QK_ROPE:]).astype(BF)
        m_s[...] = jnp.full(m_s.shape, -jnp.inf, F32)
        l_s[...] = jnp.zeros(l_s.shape, F32)
        acc_s[...] = jnp.zeros(acc_s.shape, F32)

    contract_last = (((1,), (1,)), ((), ()))

    def update(lat, s_rope, mask):
        s = lax.dot_general(ql_s[...], lat, contract_last, preferred_element_type=F32) + s_rope
        if mask is not None:
            s = jnp.where(mask, s, -jnp.inf)
        m_old = m_s[...]
        m_new = jnp.maximum(m_old, jnp.max(s, axis=-1, keepdims=True))
        alpha = jnp.exp2(m_old - m_new)
        p = jnp.exp2(s - m_new)
        l_s[...] = alpha * l_s[...] + jnp.sum(p, axis=-1, keepdims=True)
        acc_s[...] = alpha * acc_s[...] + jnp.dot(p.astype(BF), lat, preferred_element_type=F32)
        m_s[...] = m_new

    lat = jnp.concatenate([r[...].astype(BF) for r in lat_refs], axis=0)
    kr_t = jnp.concatenate([r[...].astype(BF) for r in kr_refs], axis=1)
    update(lat, jnp.dot(qr_s[...], kr_t, preferred_element_type=F32), None)

    @pl.when(jg == n_steps - 1)
    def _():
        n_pad = latn_ref.shape[0]
        tq = lax.broadcasted_iota(jnp.int32, (rows, n_pad), 0) & (t_new - 1)
        tk = lax.broadcasted_iota(jnp.int32, (rows, n_pad), 1)
        s_rope = lax.dot_general(qr_s[...], krn_ref[...], contract_last, preferred_element_type=F32)
        update(latn_ref[...], s_rope, tk <= tq)
        o_ref[...] = (acc_s[...] / l_s[...]).reshape(N_HEADS, t_new, KV_LORA)


def paged_attention(q_lat, cache_lat, cache_kr_t, page_table, lat_new, kr_new, t_new):
    bsz, n_tbl = page_table.shape
    page = cache_lat.shape[1]
    npg = PAGES_PER_STEP
    n_steps = n_tbl // npg
    rows = N_HEADS * t_new
    n_pad = lat_new.shape[1]
    qw = KV_LORA + 2 * QK_ROPE

    def page_spec(shape, p):
        return pl.BlockSpec((None,) + shape, lambda b, j, pt: (pt[b * n_tbl + j * npg + p], 0, 0))

    in_specs = [pl.BlockSpec((N_HEADS, t_new, qw), lambda b, j, pt: (0, b, 0))]
    in_specs += [page_spec((page, KV_LORA), p) for p in range(npg)]
    in_specs += [page_spec((QK_ROPE, page), p) for p in range(npg)]
    in_specs += [pl.BlockSpec((None, n_pad, KV_LORA), lambda b, j, pt: (b, 0, 0)),
                 pl.BlockSpec((None, n_pad, QK_ROPE), lambda b, j, pt: (b, 0, 0))]
    grid_spec = pltpu.PrefetchScalarGridSpec(
        num_scalar_prefetch=1,
        grid=(bsz, n_steps),
        in_specs=in_specs,
        out_specs=pl.BlockSpec((N_HEADS, t_new, KV_LORA), lambda b, j, pt: (0, b, 0)),
        scratch_shapes=[pltpu.VMEM((rows, KV_LORA), BF), pltpu.VMEM((rows, QK_ROPE), BF),
                        pltpu.VMEM((rows, 1), F32), pltpu.VMEM((rows, 1), F32),
                        pltpu.VMEM((rows, KV_LORA), F32)])
    return pl.pallas_call(
        functools.partial(_paged_kernel, n_pages=npg, n_steps=n_steps, t_new=t_new),
        grid_spec=grid_spec,
        out_shape=jax.ShapeDtypeStruct((N_HEADS, bsz * t_new, KV_LORA), F32),
        compiler_params=_cparams(2),
        name="paged_attention",
    )(page_table.reshape(-1), q_lat, *([cache_lat] * npg), *([cache_kr_t] * npg), lat_new, kr_new)


def _ov_kernel(o_ref, w_ref, out_ref):
    out_ref[...] = jnp.dot(o_ref[...].astype(BF), w_ref[...].astype(BF),
                           preferred_element_type=F32).astype(out_ref.dtype)


def latent_to_v(o_lat, w_uv2d):
    r = o_lat.shape[1]
    return pl.pallas_call(
        _ov_kernel,
        grid=(N_HEADS,),
        in_specs=[pl.BlockSpec((None, r, KV_LORA), lambda h: (h, 0, 0)),
                  pl.BlockSpec((KV_LORA, V_HEAD), lambda h: (0, h))],
        out_specs=pl.BlockSpec((r, V_HEAD), lambda h: (0, h)),
        out_shape=jax.ShapeDtypeStruct((r, N_HEADS * V_HEAD), BF),
        compiler_params=_cparams(1),
        name="latent_to_v",
    )(o_lat, w_uv2d)


def _rope_swap(w):
    half = QK_ROPE // 2
    return jnp.concatenate([-w[..., half:], w[..., :half]], axis=-1)


def _rope_tables(pos):
    half = QK_ROPE // 2
    inv = jnp.power(ROPE_THETA, -jnp.arange(half, dtype=F32) / half)
    ang = pos.astype(F32)[:, None] * inv[None, :]
    cos, sin = jnp.cos(ang), jnp.sin(ang)
    cs = jnp.concatenate([cos, cos, sin, sin], axis=-1)
    q_tab = (SM_SCALE * LOG2E) * jnp.concatenate([jnp.ones((pos.shape[0], QK_NOPE), F32), cs], axis=-1)
    return cs, q_tab


def _trunk(x3, c_mods, kv_mod, pos, conv_state, paged, W, tm):
    bsz, t_rows, d = x3.shape
    r = bsz * t_rows
    x = x3.reshape(r, d)
    is_prompt = paged is None
    tiles_per_seq = max(t_rows // tm, 1)
    cs, q_tab = _rope_tables(pos)
    if is_prompt:
        tab_idx = lambda i: i % tiles_per_seq
    else:
        cs, q_tab = jnp.tile(cs, (tm // t_rows, 1)), jnp.tile(q_tab, (tm // t_rows, 1))
        tab_idx = lambda i: 0

    def vecs(mat, n):
        return [RowVec(mat[:, i * d:(i + 1) * d], t_rows) for i in range(n)]

    layer_mods = [vecs(m, 6) for m in c_mods]
    sc_kv_sh = vecs(kv_mod, 2)

    conv_new, v_rows = [], []
    latent = k_rope = None
    kv_ctx = None
    tm_d = 1024 if is_prompt else 256
    h = modulate(x, layer_mods[0][1], layer_mods[0][0], 512 if is_prompt else 256)
    for l in range(DEPTH):
        sh1, sc1, g1, sh2, sc2, g2 = layer_mods[l]
        if l < N_A_LAYERS:
            z = mm(h, W["gm_w_in"], l, bias=W["gm_b_in"], act="gelu", tm=tm, tn=1024)
            if is_prompt:
                w_mix, b_t, bd = W["gm_w_s"][l], W["gm_b_s"][l].T, 0
            else:
                reps = CHUNK // t_rows
                w_mix = jnp.tile(W["gm_w_s"][l][:, :t_rows, :t_rows], (1, reps, reps))
                b_t = jnp.tile(W["gm_b_s"][l].T[:t_rows], (reps, 1))
                bd = t_rows
            gated, vn = spatial_gate(z, W["gm_ln_g"], W["gm_ln_b"], l, w_mix, b_t,
                                     block_diag=bd, write_v=not is_prompt)
            if vn is not None:
                v_rows.append(vn.reshape(bsz, t_rows, D_GMLP))
            x, (h,) = mm_down_ln(gated, W["gm_w_out"], l, x, g1, W["post_ln_g"], W["post_ln_b"],
                                 2 * l, [(sc2, sh2)], tm=tm_d, tk=512)
        else:
            j = l - N_A_LAYERS
            cq = mm(h, W["w_dq"], j, rms_g=W["q_norm_g"], tm=tm, tn=Q_LORA)
            q = mm(cq, W["w_uq_ext"], j, tab=q_tab, tab_idx=tab_idx, tm=tm, tn=1024, sub=HEAD_W)
            if is_prompt:
                kn, kr2, v = kv_ctx
                o = flash_prompt(q, kn, kr2, v, bsz, t_rows, 1024)
            else:
                q_lat = q_to_latent(q, W["w_uk_t"])
                lat_new, kr_new = kv_ctx
                o_lat = paged_attention(q_lat, paged[0], paged[1], paged[2], lat_new, kr_new, t_rows)
                o = latent_to_v(o_lat, W["w_uv2d"])
            x, (h,) = mm_down_ln(o, W["w_o2d"], j, x, g1, W["post_ln_g"], W["post_ln_b"],
                                 2 * l, [(sc2, sh2)], tm=tm_d, tk=512)
        gv = mm(h, W["ffn_w_up"], l, tm=tm, tn=1024)
        conv_new.append(gv.reshape(bsz, t_rows, 2 * D_FF)[:, t_rows - (CONV_W - 1):, :D_FF].astype(F32))
        mods = []
        if l + 1 < DEPTH:
            nsh1, nsc1 = layer_mods[l + 1][0], layer_mods[l + 1][1]
            mods.append((nsc1, nsh1))
        if l == N_A_LAYERS - 1:
            mods.append((sc_kv_sh[1], sc_kv_sh[0]))
        tm_f = tm_d if len(mods) < 2 or not is_prompt else tm_d // 2
        if is_prompt:
            n_tiles = r // tm_f
            last2 = gv.reshape(n_tiles, tm_f, 2 * D_FF)[:, tm_f - 2:, :D_FF].astype(F32)
            prev2 = jnp.concatenate([jnp.zeros((1, 2, D_FF), F32), last2[:-1]], axis=0)
            starts = (jnp.arange(n_tiles) * tm_f) % t_rows == 0
            prev2 = jnp.where(starts[:, None, None], 0.0, prev2)
            period = None
        else:
            prev2 = conv_state[l].astype(F32)
            period = t_rows
        p2 = jnp.pad(prev2, ((0, 0), (0, 6), (0, 0))).reshape(-1, D_FF)
        p1 = jnp.pad(prev2[:, 1:2], ((0, 0), (0, 7), (0, 0))).reshape(-1, D_FF)
        x, hs = mm_down_ln(None, W["ffn_w_down"], l, x, g2, W["post_ln_g"], W["post_ln_b"], 2 * l + 1, mods,
                           conv=(gv, p2, p1, W["ffn_conv_w"], W["ffn_conv_b"], period), tm=tm_f, tk=512)
        if hs:
            h = hs[0]
        if l == N_A_LAYERS - 1:
            latent, k_rope = dkv(hs[1], W["w_dkv_ext"], W["kv_norm_g"], cs, tab_idx, tm)
            lat_bf = latent.astype(BF)
            if is_prompt:
                kn = mm(lat_bf, W["w_uk2d"][None], 0, tm=tm, tn=1024)
                v = mm(lat_bf, W["w_uv2d"][None], 0, tm=tm, tn=1024)
                kr2 = jnp.concatenate([k_rope, k_rope], axis=-1).astype(BF)
                kv_ctx = (kn, kr2, v)
            else:
                pad = ((0, 0), (0, 16 - t_rows), (0, 0))
                kv_ctx = (jnp.pad(lat_bf.reshape(bsz, t_rows, KV_LORA), pad),
                          jnp.pad(k_rope.astype(BF).reshape(bsz, t_rows, QK_ROPE), pad))
    y = x.reshape(bsz, t_rows, d)
    return (y, latent.reshape(bsz, t_rows, KV_LORA), k_rope.reshape(bsz, t_rows, QK_ROPE),
            jnp.stack(conv_new), jnp.stack(v_rows) if v_rows else None)


def kernel(x_prompt, x_sample, cache_kv_latent, cache_k_rope, state_conv, page_table, c_prompt, c_sample,
           ada_w, ada_b, post_ln_g, post_ln_b,
           gm_w_in, gm_b_in, gm_ln_g, gm_ln_b, gm_w_s, gm_b_s, gm_w_out,
           kv_ada_w, kv_ada_b, w_dkv, kv_norm_g, w_uk, w_uv,
           w_dq, q_norm_g, w_uq, w_o,
           ffn_w_up, ffn_conv_w, ffn_conv_b, ffn_w_down):
    d = D_MODEL
    n_b = w_dq.shape[0]
    w_uq_rope = w_uq[..., QK_NOPE:]
    w_uq_ext = jnp.concatenate([w_uq[..., :QK_NOPE], w_uq_rope, _rope_swap(w_uq_rope)], axis=-1)
    w_dkv_rope = w_dkv[:, KV_LORA:]
    W = dict(
        gm_w_in=gm_w_in, gm_b_in=gm_b_in, gm_ln_g=gm_ln_g, gm_ln_b=gm_ln_b, gm_w_s=gm_w_s, gm_b_s=gm_b_s,
        gm_w_out=gm_w_out.astype(BF), post_ln_g=post_ln_g.reshape(2 * DEPTH, d), post_ln_b=post_ln_b.reshape(2 * DEPTH, d),
        w_dq=w_dq, q_norm_g=q_norm_g,
        w_uq_ext=w_uq_ext.reshape(n_b, Q_LORA, N_HEADS * HEAD_W),
        w_o2d=w_o.reshape(n_b, N_HEADS * V_HEAD, d).astype(BF),
        w_dkv_ext=jnp.concatenate([w_dkv, _rope_swap(w_dkv_rope)], axis=-1),
        kv_norm_g=kv_norm_g,
        w_uk2d=w_uk.reshape(KV_LORA, N_HEADS * QK_NOPE),
        w_uv2d=w_uv.reshape(KV_LORA, N_HEADS * V_HEAD),
        w_uk_t=jnp.transpose(w_uk, (1, 2, 0)),
        ffn_w_up=ffn_w_up, ffn_conv_w=ffn_conv_w, ffn_conv_b=ffn_conv_b, ffn_w_down=ffn_w_down.astype(BF),
    )

    n_p, n_s = c_prompt.shape[0], c_sample.shape[0]
    n_c = -(-(n_p + n_s) // 8) * 8
    c_all = jnp.pad(jnp.concatenate([c_prompt, c_sample], axis=0), ((0, n_c - n_p - n_s), (0, 0)))
    mods = [mm(c_all, ada_w, l, bias=ada_b, x_act="silu", out_dtype=F32, tm=n_c, tn=1024) for l in range(DEPTH)]
    kv_mod = mm(c_all, kv_ada_w[None], 0, bias=kv_ada_b[None], x_act="silu", out_dtype=F32, tm=n_c, tn=1024)

    seq_p = x_prompt.shape[1]
    y_p, lat_p, kr_p, conv_p, _ = _trunk(
        x_prompt, [m[:n_p] for m in mods], kv_mod[:n_p], jnp.arange(seq_p, dtype=jnp.int32),
        None, None, W, tm=1024)

    past = page_table.shape[1] * cache_kv_latent.shape[1]
    seq_s = x_sample.shape[1]
    y_s, lat_s, kr_s, conv_s, v_s = _trunk(
        x_sample, [m[n_p:n_p + n_s] for m in mods], kv_mod[n_p:n_p + n_s],
        past + jnp.arange(seq_s, dtype=jnp.int32), state_conv,
        (cache_kv_latent, jnp.swapaxes(cache_k_rope, 1, 2), page_table), W, tm=1024)

    return (y_p, y_s, lat_p, kr_p, lat_s, kr_s, conv_p, conv_s, v_s)
```

```python
import functools

import numpy as np
import jax
import jax.numpy as jnp
from jax import lax
from jax.experimental import pallas as pl
from jax.experimental.pallas import tpu as pltpu

F32 = jnp.float32
BF = jnp.bfloat16

D_MODEL = 2048
DEPTH = 4
N_A_LAYERS = DEPTH // 2
CHUNK = 128
GMLP_GROUPS = 16
D_GMLP = 3 * D_MODEL
GROUP_DIM = D_GMLP // GMLP_GROUPS
N_HEADS = 16
QK_NOPE = 128
QK_ROPE = 64
V_HEAD = 128
KV_LORA = 512
Q_LORA = 512
ROPE_THETA = 10000.0
SM_SCALE = (QK_NOPE + QK_ROPE) ** -0.5
D_FF = 5632
CONV_W = 3
ALPHA = (2 * DEPTH) ** 0.25
EPS = 1e-5
HEAD_W = 2 * QK_NOPE
PAGES_PER_STEP = 32
SEQS_PER_STEP = 1
CONV_COLS = 512
LN_ROWS = 256
LOG2E = 1.4426950408889634
VMEM_LIMIT = 60000 * 1024


def _cparams(n_axes):
    return pltpu.CompilerParams(dimension_semantics=("arbitrary",) * n_axes,
                                vmem_limit_bytes=VMEM_LIMIT)


class RowVec:
    def __init__(self, v, t_rows):
        self.b, self.d = v.shape
        self.t = t_rows
        self.expanded = t_rows < CHUNK
        self.arr = jnp.repeat(v, t_rows, axis=0) if self.expanded else v.reshape(self.b, 1, self.d)

    def spec(self, tm, row_tile):
        if self.expanded:
            return pl.BlockSpec((tm, self.d), lambda *g: (row_tile(*g), 0))
        t = self.t
        return pl.BlockSpec((None, 1, self.d), lambda *g: ((row_tile(*g) * tm) // t, 0, 0))


def _modulate_kernel(x_ref, sc_ref, sh_ref, o_ref):
    o_ref[...] = (x_ref[...] * (1.0 + sc_ref[...]) + sh_ref[...]).astype(o_ref.dtype)


def modulate(x, sc, sh, tm):
    r, d = x.shape
    row = lambda i: i
    return pl.pallas_call(
        _modulate_kernel,
        grid=(r // tm,),
        in_specs=[pl.BlockSpec((tm, d), lambda i: (i, 0)), sc.spec(tm, row), sh.spec(tm, row)],
        out_specs=pl.BlockSpec((tm, d), lambda i: (i, 0)),
        out_shape=jax.ShapeDtypeStruct((r, d), BF),
        compiler_params=_cparams(1),
        name="modulate",
    )(x, sc.arr, sh.arr)


def _mm_kernel(*refs, x_act, has_bias, act, has_rms, has_tab, sub):
    it = iter(refs)
    x_ref, w_ref = next(it), next(it)
    b_ref = next(it) if has_bias else None
    g_ref = next(it) if has_rms else None
    t_ref = next(it) if has_tab else None
    o_ref, wb = next(it), next(it)

    @pl.when(pl.program_id(1) == 0)
    def _():
        wb[...] = w_ref[...].astype(BF)

    xv = x_ref[...]
    if x_act == "silu":
        xf = xv.astype(F32)
        xv = xf * jax.nn.sigmoid(xf)
    xb = xv.astype(BF)
    tn = o_ref.shape[1]
    for c in range(tn // sub):
        sl = slice(c * sub, (c + 1) * sub)
        acc = jnp.dot(xb, wb[:, sl], preferred_element_type=F32)
        if has_bias:
            acc = acc + b_ref[:, sl]
        if act == "gelu":
            acc = 0.5 * acc * (1.0 + lax.erf(acc * np.float32(2.0 ** -0.5)))
        if has_rms:
            acc = acc * lax.rsqrt(jnp.mean(acc * acc, axis=-1, keepdims=True) + EPS) * g_ref[...]
        if has_tab:
            acc = acc * t_ref[...]
        o_ref[:, sl] = acc.astype(o_ref.dtype)


def mm(x, w, l, *, bias=None, x_act=None, act=None, rms_g=None, tab=None, tab_idx=None,
       out_dtype=BF, tm, tn, sub=256):
    m, k = x.shape
    n = w.shape[2]
    sub = min(sub, tn)
    if rms_g is not None:
        assert tn == n
        sub = tn
    in_specs = [pl.BlockSpec((tm, k), lambda j, i: (i, 0)),
                pl.BlockSpec((None, k, tn), lambda j, i: (l, 0, j))]
    args = [x, w]
    for v in (bias, rms_g):
        if v is not None:
            in_specs.append(pl.BlockSpec((None, 1, tn), lambda j, i: (l, 0, j)))
            args.append(v.reshape(v.shape[0], 1, n))
    if tab is not None:
        assert tab.shape[1] == sub
        in_specs.append(pl.BlockSpec((tm, sub), lambda j, i: (tab_idx(i), 0)))
        args.append(tab)
    kern = functools.partial(_mm_kernel, x_act=x_act, has_bias=bias is not None, act=act,
                             has_rms=rms_g is not None, has_tab=tab is not None, sub=sub)
    return pl.pallas_call(
        kern,
        grid=(n // tn, m // tm),
        in_specs=in_specs,
        out_specs=pl.BlockSpec((tm, tn), lambda j, i: (i, j)),
        out_shape=jax.ShapeDtypeStruct((m, n), out_dtype),
        scratch_shapes=[pltpu.VMEM((k, tn), BF)],
        compiler_params=_cparams(2),
        name="mm",
    )(*args)


def _gate_kernel(u_ref, v_ref, lg_ref, lb_ref, w_ref, bt_ref, *out_refs, block_diag, write_v):
    o_ref = out_refs[0]
    v = v_ref[...].astype(F32)
    mu = jnp.mean(v, axis=-1, keepdims=True)
    dv = v - mu
    var = jnp.mean(dv * dv, axis=-1, keepdims=True)
    vn = dv * lax.rsqrt(var + EPS) * lg_ref[...] + lb_ref[...]
    if write_v:
        out_refs[1][...] = vn
    vb = vn.astype(BF)
    row = lax.broadcasted_iota(jnp.int32, (CHUNK, CHUNK), 0)
    col = lax.broadcasted_iota(jnp.int32, (CHUNK, CHUNK), 1)
    mask = row >= col
    if block_diag:
        mask = jnp.logical_and(mask, (row // block_diag) == (col // block_diag))
    for g in range(GMLP_GROUPS):
        sl = slice(g * GROUP_DIM, (g + 1) * GROUP_DIM)
        wm = jnp.where(mask, w_ref[g], 0.0).astype(BF)
        s = jnp.dot(wm, vb[:, sl], preferred_element_type=F32) + bt_ref[:, g:g + 1]
        o_ref[:, sl] = (u_ref[:, sl].astype(F32) * s).astype(o_ref.dtype)


def spatial_gate(z, ln_g, ln_b, l, w_mix, b_t, *, block_diag, write_v):
    r = z.shape[0]
    nb = D_GMLP // D_GMLP
    out_shape = [jax.ShapeDtypeStruct((r, D_GMLP), BF)]
    out_specs = [pl.BlockSpec((CHUNK, D_GMLP), lambda c: (c, 0))]
    if write_v:
        out_shape.append(jax.ShapeDtypeStruct((r, D_GMLP), F32))
        out_specs.append(pl.BlockSpec((CHUNK, D_GMLP), lambda c: (c, 0)))
    res = pl.pallas_call(
        functools.partial(_gate_kernel, block_diag=block_diag, write_v=write_v),
        grid=(r // CHUNK,),
        in_specs=[pl.BlockSpec((CHUNK, D_GMLP), lambda c: (c, 0)),
                  pl.BlockSpec((CHUNK, D_GMLP), lambda c: (c, nb)),
                  pl.BlockSpec((None, 1, D_GMLP), lambda c: (l, 0, 0)),
                  pl.BlockSpec((None, 1, D_GMLP), lambda c: (l, 0, 0)),
                  pl.BlockSpec((GMLP_GROUPS, CHUNK, CHUNK), lambda c: (0, 0, 0)),
                  pl.BlockSpec((CHUNK, GMLP_GROUPS), lambda c: (0, 0))],
        out_specs=out_specs,
        out_shape=out_shape,
        compiler_params=_cparams(1),
        name="spatial_gate",
    )(z, z, ln_g.reshape(-1, 1, D_GMLP), ln_b.reshape(-1, 1, D_GMLP), w_mix, b_t)
    return res if write_v else (res[0], None)


def _down_kernel(*refs, conv_period, n_mod, nk, kc):
    it = iter(refs)
    if conv_period:
        g_ref, v_ref, p2_ref, p1_ref, cw_ref, cb_ref = (next(it) for _ in range(6))
    else:
        a_ref = next(it)
    w_ref, res_ref, gate_ref, pg_ref, pb_ref = (next(it) for _ in range(5))
    mod_refs = [(next(it), next(it)) for _ in range(n_mod)]
    xo_ref = next(it)
    h_refs = [next(it) for _ in range(n_mod)]
    tm = xo_ref.shape[0]
    tk = w_ref.shape[0]

    def conv_act(sl):
        g = g_ref[:, sl].astype(F32)
        t = lax.broadcasted_iota(jnp.int32, g.shape, 0) & (conv_period - 1)
        p2, p1 = p2_ref[:, sl], p1_ref[:, sl]
        if p2.shape[0] != tm:
            p2 = jnp.tile(p2, (tm // p2.shape[0], 1))
            p1 = jnp.tile(p1, (tm // p1.shape[0], 1))
        gm2 = jnp.where(t < 2, p2, pltpu.roll(g, 2, 0))
        gm1 = jnp.where(t < 1, p1, pltpu.roll(g, 1, 0))
        gc = cb_ref[:, sl] + cw_ref[0:1, sl] * gm2 + cw_ref[1:2, sl] * gm1 + cw_ref[2:3, sl] * g
        return (gc * jax.nn.sigmoid(gc) * v_ref[:, sl].astype(F32)).astype(BF)

    if conv_period:
        d = None
        for c in range(tk // kc):
            sl = slice(c * kc, (c + 1) * kc)
            dc = jnp.dot(conv_act(sl), w_ref[sl, :], preferred_element_type=F32)
            d = dc if d is None else d + dc
    else:
        d = jnp.dot(a_ref[...], w_ref[...], preferred_element_type=F32)

    def epilogue(rc, acc_rows):
        def rows(ref, r0):
            return ref[pl.ds(r0, rc), :] if ref.shape[0] == tm else ref[...]

        def chunk(c, carry):
            r0 = pl.multiple_of(c * rc, rc)
            y = ALPHA * res_ref[pl.ds(r0, rc), :] + rows(gate_ref, r0) * acc_rows(r0)
            mu = jnp.mean(y, axis=-1, keepdims=True)
            dy = y - mu
            var = jnp.mean(dy * dy, axis=-1, keepdims=True)
            yn = dy * lax.rsqrt(var + EPS) * pg_ref[...] + pb_ref[...]
            xo_ref[pl.ds(r0, rc), :] = yn
            for (sc_ref, sh_ref), h_ref in zip(mod_refs, h_refs):
                h_ref[pl.ds(r0, rc), :] = (yn * (1.0 + rows(sc_ref, r0)) + rows(sh_ref, r0)).astype(h_ref.dtype)
            return carry

        lax.fori_loop(0, tm // rc, chunk, 0)

    if nk == 1:
        epilogue(tm, lambda r0: d)
    else:
        k = pl.program_id(1)

        @pl.when(k == 0)
        def _():
            xo_ref[...] = d

        @pl.when(k > 0)
        def _():
            xo_ref[...] += d

        @pl.when(k == nk - 1)
        def _():
            rc = min(tm, LN_ROWS)
            epilogue(rc, lambda r0: xo_ref[pl.ds(r0, rc), :])


def mm_down_ln(a, w, l, resid, gate, post_g, post_b, ln_idx, mods, *, conv=None, tm, tk=None):
    r = resid.shape[0]
    k_dim, n = w.shape[1], w.shape[2]
    tk = k_dim if tk is None else tk
    nk = k_dim // tk
    row = lambda i, k: i
    in_specs, args = [], []
    if conv is not None:
        gv, p2, p1, cw, cb, period = conv
        pr = p2.shape[0] // (r // tm)
        in_specs += [pl.BlockSpec((tm, tk), lambda i, k: (i, k)),
                     pl.BlockSpec((tm, tk), lambda i, k: (i, k + nk)),
                     pl.BlockSpec((pr, tk), lambda i, k: (i, k)),
                     pl.BlockSpec((pr, tk), lambda i, k: (i, k)),
                     pl.BlockSpec((None, CONV_W, tk), lambda i, k: (l, 0, k)),
                     pl.BlockSpec((None, 1, tk), lambda i, k: (l, 0, k))]
        args += [gv, gv, p2, p1, cw, cb.reshape(cb.shape[0], 1, k_dim)]
        conv_period = tm if period is None else period
    else:
        in_specs.append(pl.BlockSpec((tm, tk), lambda i, k: (i, k)))
        args.append(a)
        conv_period = 0
    w_mode = dict(pipeline_mode=pl.Buffered(1)) if nk == 1 else {}
    in_specs += [pl.BlockSpec((None, tk, n), lambda i, k: (l, k, 0), **w_mode),
                 pl.BlockSpec((tm, n), lambda i, k: (i, 0)),
                 gate.spec(tm, row),
                 pl.BlockSpec((None, 1, n), lambda i, k: (ln_idx, 0, 0)),
                 pl.BlockSpec((None, 1, n), lambda i, k: (ln_idx, 0, 0))]
    args += [w, resid, gate.arr, post_g.reshape(-1, 1, n), post_b.reshape(-1, 1, n)]
    for sc, sh in mods:
        in_specs += [sc.spec(tm, row), sh.spec(tm, row)]
        args += [sc.arr, sh.arr]
    out_shape = [jax.ShapeDtypeStruct((r, n), F32)] + [jax.ShapeDtypeStruct((r, n), BF)] * len(mods)
    out_specs = [pl.BlockSpec((tm, n), lambda i, k: (i, 0)) for _ in out_shape]
    res = pl.pallas_call(
        functools.partial(_down_kernel, conv_period=conv_period, n_mod=len(mods), nk=nk, kc=min(tk, CONV_COLS)),
        grid=(r // tm, nk),
        in_specs=in_specs,
        out_specs=out_specs,
        out_shape=out_shape,
        compiler_params=_cparams(2),
        name="mm_down_ln",
    )(*args)
    return res[0], list(res[1:])


def _dkv_kernel(x_ref, w_ref, g_ref, cs_ref, lat_ref, kr_ref):
    a = jnp.dot(x_ref[...], w_ref[...].astype(BF), preferred_element_type=F32)
    lat = a[:, :KV_LORA]
    lat_ref[...] = lat * lax.rsqrt(jnp.mean(lat * lat, axis=-1, keepdims=True) + EPS) * g_ref[...]
    t = a[:, KV_LORA:] * cs_ref[...]
    kr_ref[...] = t[:, :QK_ROPE] + t[:, QK_ROPE:]


def dkv(x, w_ext, norm_g, cs, cs_idx, tm):
    r, k = x.shape
    n = w_ext.shape[1]
    return pl.pallas_call(
        _dkv_kernel,
        grid=(r // tm,),
        in_specs=[pl.BlockSpec((tm, k), lambda i: (i, 0)),
                  pl.BlockSpec((k, n), lambda i: (0, 0)),
                  pl.BlockSpec((1, KV_LORA), lambda i: (0, 0)),
                  pl.BlockSpec((tm, 2 * QK_ROPE), lambda i: (cs_idx(i), 0))],
        out_specs=[pl.BlockSpec((tm, KV_LORA), lambda i: (i, 0)),
                   pl.BlockSpec((tm, QK_ROPE), lambda i: (i, 0))],
        out_shape=[jax.ShapeDtypeStruct((r, KV_LORA), F32), jax.ShapeDtypeStruct((r, QK_ROPE), F32)],
        compiler_params=_cparams(1),
        name="dkv",
    )(x, w_ext, norm_g.reshape(1, KV_LORA), cs)


def _flash_kernel(q_ref, kn_ref, kr_ref, v_ref, o_ref, *, tq):
    qi = pl.program_id(2)
    q = q_ref[...]

    def step(j, carry, masked):
        m, l, acc = carry
        ks = pl.multiple_of(j * tq, tq)
        kk = jnp.concatenate([kn_ref[pl.ds(ks, tq), :], kr_ref[pl.ds(ks, tq), :]], axis=1)
        s = lax.dot_general(q, kk, (((1,), (1,)), ((), ())), preferred_element_type=F32)
        if masked:
            row = lax.broadcasted_iota(jnp.int32, (tq, tq), 0)
            col = lax.broadcasted_iota(jnp.int32, (tq, tq), 1)
            s = jnp.where(row >= col, s, -jnp.inf)
        m_new = jnp.maximum(m, jnp.max(s, axis=-1, keepdims=True))
        alpha = jnp.exp2(m - m_new)
        p = jnp.exp2(s - m_new)
        l = alpha * l + jnp.sum(p, axis=-1, keepdims=True)
        acc = alpha * acc + jnp.dot(p.astype(BF), v_ref[pl.ds(ks, tq), :], preferred_element_type=F32)
        return m_new, l, acc

    init = (jnp.full((tq, 1), -jnp.inf, F32), jnp.zeros((tq, 1), F32), jnp.zeros((tq, V_HEAD), F32))
    carry = lax.fori_loop(0, qi, lambda j, c: step(j, c, False), init)
    _, l, acc = step(qi, carry, True)
    o_ref[...] = (acc / l).astype(o_ref.dtype)


def flash_prompt(q, kn, kr2, v, bsz, seq, tq):
    r = q.shape[0]
    nq = seq // tq
    return pl.pallas_call(
        functools.partial(_flash_kernel, tq=tq),
        grid=(bsz, N_HEADS, nq),
        in_specs=[pl.BlockSpec((tq, HEAD_W), lambda b, h, i: (b * nq + i, h)),
                  pl.BlockSpec((seq, QK_NOPE), lambda b, h, i: (b, h)),
                  pl.BlockSpec((seq, 2 * QK_ROPE), lambda b, h, i: (b, 0)),
                  pl.BlockSpec((seq, V_HEAD), lambda b, h, i: (b, h))],
        out_specs=pl.BlockSpec((tq, V_HEAD), lambda b, h, i: (b * nq + i, h)),
        out_shape=jax.ShapeDtypeStruct((r, N_HEADS * V_HEAD), BF),
        compiler_params=_cparams(3),
        name="flash_prompt",
    )(q, kn, kr2, v)


def _qlat_kernel(x_ref, w_ref, o_ref):
    x = x_ref[...]
    o_ref[:, :KV_LORA] = jnp.dot(x[:, :QK_NOPE], w_ref[...].astype(BF), preferred_element_type=F32)
    o_ref[:, KV_LORA:] = x[:, QK_NOPE:].astype(F32)


def q_to_latent(q, w_uk_t):
    r = q.shape[0]
    return pl.pallas_call(
        _qlat_kernel,
        grid=(N_HEADS,),
        in_specs=[pl.BlockSpec((r, HEAD_W), lambda h: (0, h)),
                  pl.BlockSpec((None, QK_NOPE, KV_LORA), lambda h: (h, 0, 0))],
        out_specs=pl.BlockSpec((None, r, KV_LORA + 2 * QK_ROPE), lambda h: (h, 0, 0)),
        out_shape=jax.ShapeDtypeStruct((N_HEADS, r, KV_LORA + 2 * QK_ROPE), F32),
        compiler_params=_cparams(1),
        name="q_to_latent",
    )(q, w_uk_t)


def _paged_kernel(pt_ref, q_ref, *refs, n_seq, n_pages, n_steps, t_new):
    n_in = n_seq * n_pages
    lat_refs = refs[:n_in]
    kr_refs = refs[n_in:2 * n_in]
    latn_ref, krn_ref, o_ref, ql_s, qr_s, m_s, l_s, acc_s = refs[2 * n_in:]
    jg = pl.program_id(1)
    rows = N_HEADS * t_new
    contract_last = (((1,), (1,)), ((), ()))

    @pl.when(jg == 0)
    def _():
        for sq in range(n_seq):
            q = q_ref[:, sq * t_new:(sq + 1) * t_new, :].reshape(rows, KV_LORA + 2 * QK_ROPE)
            ql_s[sq] = q[:, :KV_LORA].astype(BF)
            qr_s[sq] = (q[:, KV_LORA:KV_LORA + QK_ROPE] + q[:, KV_LORA + QK_ROPE:]).astype(BF)
            m_s[sq] = jnp.full((rows, 1), -jnp.inf, F32)
            l_s[sq] = jnp.zeros((rows, 1), F32)
            acc_s[sq] = jnp.zeros((rows, KV_LORA), F32)

    def update(sq, lat, s_rope, mask):
        s = lax.dot_general(ql_s[sq], lat, contract_last, preferred_element_type=F32) + s_rope
        if mask is not None:
            s = jnp.where(mask, s, -jnp.inf)
        m_old = m_s[sq]
        m_new = jnp.maximum(m_old, jnp.max(s, axis=-1, keepdims=True))
        alpha = jnp.exp2(m_old - m_new)
        p = jnp.exp2(s - m_new)
        l_s[sq] = alpha * l_s[sq] + jnp.sum(p, axis=-1, keepdims=True)
        acc_s[sq] = alpha * acc_s[sq] + jnp.dot(p.astype(BF), lat, preferred_element_type=F32)
        m_s[sq] = m_new

    for sq in range(n_seq):
        pages = slice(sq * n_pages, (sq + 1) * n_pages)
        lat = jnp.concatenate([r[...].astype(BF) for r in lat_refs[pages]], axis=0)
        kr_t = jnp.concatenate([r[...].astype(BF) for r in kr_refs[pages]], axis=1)
        update(sq, lat, jnp.dot(qr_s[sq], kr_t, preferred_element_type=F32), None)

    @pl.when(jg == n_steps - 1)
    def _():
        n_pad = latn_ref.shape[1]
        tq = lax.broadcasted_iota(jnp.int32, (rows, n_pad), 0) & (t_new - 1)
        tk = lax.broadcasted_iota(jnp.int32, (rows, n_pad), 1)
        for sq in range(n_seq):
            s_rope = lax.dot_general(qr_s[sq], krn_ref[sq], contract_last, preferred_element_type=F32)
            update(sq, latn_ref[sq], s_rope, tk <= tq)
            o_ref[:, sq * t_new:(sq + 1) * t_new, :] = (acc_s[sq] / l_s[sq]).reshape(N_HEADS, t_new, KV_LORA)


def paged_attention(q_lat, cache_lat, cache_kr_t, page_table, lat_new, kr_new, t_new):
    bsz, n_tbl = page_table.shape
    page = cache_lat.shape[1]
    npg, nsq = PAGES_PER_STEP, SEQS_PER_STEP
    n_steps = n_tbl // npg
    rows = N_HEADS * t_new
    n_pad = lat_new.shape[1]
    qw = KV_LORA + 2 * QK_ROPE

    def page_spec(shape, sq, p):
        return pl.BlockSpec((None,) + shape,
                            lambda b, j, pt: (pt[(b * nsq + sq) * n_tbl + j * npg + p], 0, 0))

    in_specs = [pl.BlockSpec((N_HEADS, nsq * t_new, qw), lambda b, j, pt: (0, b, 0))]
    in_specs += [page_spec((page, KV_LORA), sq, p) for sq in range(nsq) for p in range(npg)]
    in_specs += [page_spec((QK_ROPE, page), sq, p) for sq in range(nsq) for p in range(npg)]
    in_specs += [pl.BlockSpec((nsq, n_pad, KV_LORA), lambda b, j, pt: (b, 0, 0)),
                 pl.BlockSpec((nsq, n_pad, QK_ROPE), lambda b, j, pt: (b, 0, 0))]
    grid_spec = pltpu.PrefetchScalarGridSpec(
        num_scalar_prefetch=1,
        grid=(bsz // nsq, n_steps),
        in_specs=in_specs,
        out_specs=pl.BlockSpec((N_HEADS, nsq * t_new, KV_LORA), lambda b, j, pt: (0, b, 0)),
        scratch_shapes=[pltpu.VMEM((nsq, rows, KV_LORA), BF), pltpu.VMEM((nsq, rows, QK_ROPE), BF),
                        pltpu.VMEM((nsq, rows, 1), F32), pltpu.VMEM((nsq, rows, 1), F32),
                        pltpu.VMEM((nsq, rows, KV_LORA), F32)])
    n_in = nsq * npg
    return pl.pallas_call(
        functools.partial(_paged_kernel, n_seq=nsq, n_pages=npg, n_steps=n_steps, t_new=t_new),
        grid_spec=grid_spec,
        out_shape=jax.ShapeDtypeStruct((N_HEADS, bsz * t_new, KV_LORA), F32),
        compiler_params=_cparams(2),
        name="paged_attention",
    )(page_table.reshape(-1), q_lat, *([cache_lat] * n_in), *([cache_kr_t] * n_in), lat_new, kr_new)


def _ov_kernel(o_ref, w_ref, out_ref):
    out_ref[...] = jnp.dot(o_ref[...].astype(BF), w_ref[...].astype(BF),
                           preferred_element_type=F32).astype(out_ref.dtype)


def latent_to_v(o_lat, w_uv2d):
    r = o_lat.shape[1]
    return pl.pallas_call(
        _ov_kernel,
        grid=(N_HEADS,),
        in_specs=[pl.BlockSpec((None, r, KV_LORA), lambda h: (h, 0, 0)),
                  pl.BlockSpec((KV_LORA, V_HEAD), lambda h: (0, h))],
        out_specs=pl.BlockSpec((r, V_HEAD), lambda h: (0, h)),
        out_shape=jax.ShapeDtypeStruct((r, N_HEADS * V_HEAD), BF),
        compiler_params=_cparams(1),
        name="latent_to_v",
    )(o_lat, w_uv2d)


def _rope_swap(w):
    half = QK_ROPE // 2
    return jnp.concatenate([-w[..., half:], w[..., :half]], axis=-1)


def _rope_tables(pos):
    half = QK_ROPE // 2
    inv = jnp.power(ROPE_THETA, -jnp.arange(half, dtype=F32) / half)
    ang = pos.astype(F32)[:, None] * inv[None, :]
    cos, sin = jnp.cos(ang), jnp.sin(ang)
    cs = jnp.concatenate([cos, cos, sin, sin], axis=-1)
    q_tab = (SM_SCALE * LOG2E) * jnp.concatenate([jnp.ones((pos.shape[0], QK_NOPE), F32), cs], axis=-1)
    return cs, q_tab


def _trunk(x3, c_mods, kv_mod, pos, conv_state, paged, W, tm):
    bsz, t_rows, d = x3.shape
    r = bsz * t_rows
    x = x3.reshape(r, d)
    is_prompt = paged is None
    tiles_per_seq = max(t_rows // tm, 1)
    cs, q_tab = _rope_tables(pos)
    if is_prompt:
        tab_idx = lambda i: i % tiles_per_seq
    else:
        cs, q_tab = jnp.tile(cs, (tm // t_rows, 1)), jnp.tile(q_tab, (tm // t_rows, 1))
        tab_idx = lambda i: 0

    def vecs(mat, n):
        return [RowVec(mat[:, i * d:(i + 1) * d], t_rows) for i in range(n)]

    layer_mods = [vecs(m, 6) for m in c_mods]
    sc_kv_sh = vecs(kv_mod, 2)

    conv_new, v_rows = [], []
    latent = k_rope = None
    kv_ctx = None
    tm_d = 256
    tk_d = None if is_prompt else 512
    h = modulate(x, layer_mods[0][1], layer_mods[0][0], 512 if is_prompt else 256)
    for l in range(DEPTH):
        sh1, sc1, g1, sh2, sc2, g2 = layer_mods[l]
        if l < N_A_LAYERS:
            z = mm(h, W["gm_w_in"], l, bias=W["gm_b_in"], act="gelu", tm=tm, tn=1024)
            if is_prompt:
                w_mix, b_t, bd = W["gm_w_s"][l], W["gm_b_s"][l].T, 0
            else:
                reps = CHUNK // t_rows
                w_mix = jnp.tile(W["gm_w_s"][l][:, :t_rows, :t_rows], (1, reps, reps))
                b_t = jnp.tile(W["gm_b_s"][l].T[:t_rows], (reps, 1))
                bd = t_rows
            gated, vn = spatial_gate(z, W["gm_ln_g"], W["gm_ln_b"], l, w_mix, b_t,
                                     block_diag=bd, write_v=not is_prompt)
            if vn is not None:
                v_rows.append(vn.reshape(bsz, t_rows, D_GMLP))
            x, (h,) = mm_down_ln(gated, W["gm_w_out"], l, x, g1, W["post_ln_g"], W["post_ln_b"],
                                 2 * l, [(sc2, sh2)], tm=tm_d, tk=tk_d)
        else:
            j = l - N_A_LAYERS
            cq = mm(h, W["w_dq"], j, rms_g=W["q_norm_g"], tm=tm, tn=Q_LORA)
            q = mm(cq, W["w_uq_ext"], j, tab=q_tab, tab_idx=tab_idx, tm=tm, tn=1024, sub=HEAD_W)
            if is_prompt:
                kn, kr2, v = kv_ctx
                o = flash_prompt(q, kn, kr2, v, bsz, t_rows, 1024)
            else:
                q_lat = q_to_latent(q, W["w_uk_t"])
                lat_new, kr_new = kv_ctx
                o_lat = paged_attention(q_lat, paged[0], paged[1], paged[2], lat_new, kr_new, t_rows)
                o = latent_to_v(o_lat, W["w_uv2d"])
            x, (h,) = mm_down_ln(o, W["w_o2d"], j, x, g1, W["post_ln_g"], W["post_ln_b"],
                                 2 * l, [(sc2, sh2)], tm=tm_d, tk=tk_d)
        gv = mm(h, W["ffn_w_up"], l, tm=tm, tn=1024)
        conv_new.append(gv.reshape(bsz, t_rows, 2 * D_FF)[:, t_rows - (CONV_W - 1):, :D_FF].astype(F32))
        mods = []
        if l + 1 < DEPTH:
            nsh1, nsc1 = layer_mods[l + 1][0], layer_mods[l + 1][1]
            mods.append((nsc1, nsh1))
        if l == N_A_LAYERS - 1:
            mods.append((sc_kv_sh[1], sc_kv_sh[0]))
        tm_f = tm_d
        if is_prompt:
            n_tiles = r // tm_f
            last2 = gv.reshape(n_tiles, tm_f, 2 * D_FF)[:, tm_f - 2:, :D_FF].astype(F32)
            prev2 = jnp.concatenate([jnp.zeros((1, 2, D_FF), F32), last2[:-1]], axis=0)
            starts = (jnp.arange(n_tiles) * tm_f) % t_rows == 0
            prev2 = jnp.where(starts[:, None, None], 0.0, prev2)
            period = None
        else:
            prev2 = conv_state[l].astype(F32)
            period = t_rows
        p2 = jnp.pad(prev2, ((0, 0), (0, 6), (0, 0))).reshape(-1, D_FF)
        p1 = jnp.pad(prev2[:, 1:2], ((0, 0), (0, 7), (0, 0))).reshape(-1, D_FF)
        x, hs = mm_down_ln(None, W["ffn_w_down"], l, x, g2, W["post_ln_g"], W["post_ln_b"], 2 * l + 1, mods,
                           conv=(gv, p2, p1, W["ffn_conv_w"], W["ffn_conv_b"], period), tm=tm_f, tk=tk_d)
        if hs:
            h = hs[0]
        if l == N_A_LAYERS - 1:
            latent, k_rope = dkv(hs[1], W["w_dkv_ext"], W["kv_norm_g"], cs, tab_idx, tm)
            lat_bf = latent.astype(BF)
            if is_prompt:
                kn = mm(lat_bf, W["w_uk2d"][None], 0, tm=tm, tn=1024)
                v = mm(lat_bf, W["w_uv2d"][None], 0, tm=tm, tn=1024)
                kr2 = jnp.concatenate([k_rope, k_rope], axis=-1).astype(BF)
                kv_ctx = (kn, kr2, v)
            else:
                pad = ((0, 0), (0, 16 - t_rows), (0, 0))
                kv_ctx = (jnp.pad(lat_bf.reshape(bsz, t_rows, KV_LORA), pad),
                          jnp.pad(k_rope.astype(BF).reshape(bsz, t_rows, QK_ROPE), pad))
    y = x.reshape(bsz, t_rows, d)
    return (y, latent.reshape(bsz, t_rows, KV_LORA), k_rope.reshape(bsz, t_rows, QK_ROPE),
            jnp.stack(conv_new), jnp.stack(v_rows) if v_rows else None)


def kernel(x_prompt, x_sample, cache_kv_latent, cache_k_rope, state_conv, page_table, c_prompt, c_sample,
           ada_w, ada_b, post_ln_g, post_ln_b,
           gm_w_in, gm_b_in, gm_ln_g, gm_ln_b, gm_w_s, gm_b_s, gm_w_out,
           kv_ada_w, kv_ada_b, w_dkv, kv_norm_g, w_uk, w_uv,
           w_dq, q_norm_g, w_uq, w_o,
           ffn_w_up, ffn_conv_w, ffn_conv_b, ffn_w_down):
    d = D_MODEL
    n_b = w_dq.shape[0]
    w_uq_rope = w_uq[..., QK_NOPE:]
    w_uq_ext = jnp.concatenate([w_uq[..., :QK_NOPE], w_uq_rope, _rope_swap(w_uq_rope)], axis=-1)
    w_dkv_rope = w_dkv[:, KV_LORA:]
    W = dict(
        gm_w_in=gm_w_in, gm_b_in=gm_b_in, gm_ln_g=gm_ln_g, gm_ln_b=gm_ln_b, gm_w_s=gm_w_s, gm_b_s=gm_b_s,
        gm_w_out=gm_w_out.astype(BF), post_ln_g=post_ln_g.reshape(2 * DEPTH, d), post_ln_b=post_ln_b.reshape(2 * DEPTH, d),
        w_dq=w_dq, q_norm_g=q_norm_g,
        w_uq_ext=w_uq_ext.reshape(n_b, Q_LORA, N_HEADS * HEAD_W),
        w_o2d=w_o.reshape(n_b, N_HEADS * V_HEAD, d).astype(BF),
        w_dkv_ext=jnp.concatenate([w_dkv, _rope_swap(w_dkv_rope)], axis=-1),
        kv_norm_g=kv_norm_g,
        w_uk2d=w_uk.reshape(KV_LORA, N_HEADS * QK_NOPE),
        w_uv2d=w_uv.reshape(KV_LORA, N_HEADS * V_HEAD),
        w_uk_t=jnp.transpose(w_uk, (1, 2, 0)),
        ffn_w_up=ffn_w_up, ffn_conv_w=ffn_conv_w, ffn_conv_b=ffn_conv_b, ffn_w_down=ffn_w_down.astype(BF),
    )

    n_p, n_s = c_prompt.shape[0], c_sample.shape[0]
    n_c = -(-(n_p + n_s) // 8) * 8
    c_all = jnp.pad(jnp.concatenate([c_prompt, c_sample], axis=0), ((0, n_c - n_p - n_s), (0, 0)))
    mods = [mm(c_all, ada_w, l, bias=ada_b, x_act="silu", out_dtype=F32, tm=n_c, tn=1024) for l in range(DEPTH)]
    kv_mod = mm(c_all, kv_ada_w[None], 0, bias=kv_ada_b[None], x_act="silu", out_dtype=F32, tm=n_c, tn=1024)

    seq_p = x_prompt.shape[1]
    y_p, lat_p, kr_p, conv_p, _ = _trunk(
        x_prompt, [m[:n_p] for m in mods], kv_mod[:n_p], jnp.arange(seq_p, dtype=jnp.int32),
        None, None, W, tm=1024)

    past = page_table.shape[1] * cache_kv_latent.shape[1]
    seq_s = x_sample.shape[1]
    y_s, lat_s, kr_s, conv_s, v_s = _trunk(
        x_sample, [m[n_p:n_p + n_s] for m in mods], kv_mod[n_p:n_p + n_s],
        past + jnp.arange(seq_s, dtype=jnp.int32), state_conv,
        (cache_kv_latent, jnp.swapaxes(cache_k_rope, 1, 2), page_table), W, tm=1024)

    return (y_p, y_s, lat_p, kr_p, lat_s, kr_s, conv_p, conv_s, v_s)
```

```python
import functools

import numpy as np
import jax
import jax.numpy as jnp
from jax import lax
from jax.experimental import pallas as pl
from jax.experimental.pallas import tpu as pltpu

F32 = jnp.float32
BF = jnp.bfloat16

D_MODEL = 2048
DEPTH = 4
N_A_LAYERS = DEPTH // 2
CHUNK = 128
GMLP_GROUPS = 16
D_GMLP = 3 * D_MODEL
GROUP_DIM = D_GMLP // GMLP_GROUPS
N_HEADS = 16
QK_NOPE = 128
QK_ROPE = 64
V_HEAD = 128
KV_LORA = 512
Q_LORA = 512
ROPE_THETA = 10000.0
SM_SCALE = (QK_NOPE + QK_ROPE) ** -0.5
D_FF = 5632
CONV_W = 3
ALPHA = (2 * DEPTH) ** 0.25
EPS = 1e-5
HEAD_W = 2 * QK_NOPE
PAGES_PER_STEP = 32
SEQS_PER_STEP = 1
CONV_COLS = 512
LN_ROWS = 256
LOG2E = 1.4426950408889634
VMEM_LIMIT = 60000 * 1024


def _cparams(n_axes):
    return pltpu.CompilerParams(dimension_semantics=("arbitrary",) * n_axes,
                                vmem_limit_bytes=VMEM_LIMIT)


class RowVec:
    def __init__(self, v, t_rows):
        self.b, self.d = v.shape
        self.t = t_rows
        self.expanded = t_rows < CHUNK
        self.arr = jnp.repeat(v, t_rows, axis=0) if self.expanded else v.reshape(self.b, 1, self.d)

    def spec(self, tm, row_tile):
        if self.expanded:
            return pl.BlockSpec((tm, self.d), lambda *g: (row_tile(*g), 0))
        t = self.t
        return pl.BlockSpec((None, 1, self.d), lambda *g: ((row_tile(*g) * tm) // t, 0, 0))


def _modulate_kernel(x_ref, sc_ref, sh_ref, o_ref):
    o_ref[...] = (x_ref[...] * (1.0 + sc_ref[...]) + sh_ref[...]).astype(o_ref.dtype)


def modulate(x, sc, sh, tm):
    r, d = x.shape
    row = lambda i: i
    return pl.pallas_call(
        _modulate_kernel,
        grid=(r // tm,),
        in_specs=[pl.BlockSpec((tm, d), lambda i: (i, 0)), sc.spec(tm, row), sh.spec(tm, row)],
        out_specs=pl.BlockSpec((tm, d), lambda i: (i, 0)),
        out_shape=jax.ShapeDtypeStruct((r, d), BF),
        compiler_params=_cparams(1),
        name="modulate",
    )(x, sc.arr, sh.arr)


def _mm_kernel(*refs, x_act, has_bias, act, has_rms, has_tab, sub):
    it = iter(refs)
    x_ref, w_ref = next(it), next(it)
    b_ref = next(it) if has_bias else None
    g_ref = next(it) if has_rms else None
    t_ref = next(it) if has_tab else None
    o_ref, wb = next(it), next(it)

    @pl.when(pl.program_id(1) == 0)
    def _():
        wb[...] = w_ref[...].astype(BF)

    xv = x_ref[...]
    if x_act == "silu":
        xf = xv.astype(F32)
        xv = xf * jax.nn.sigmoid(xf)
    xb = xv.astype(BF)
    tn = o_ref.shape[1]
    for c in range(tn // sub):
        sl = slice(c * sub, (c + 1) * sub)
        acc = jnp.dot(xb, wb[:, sl], preferred_element_type=F32)
        if has_bias:
            acc = acc + b_ref[:, sl]
        if act == "gelu":
            acc = 0.5 * acc * (1.0 + lax.erf(acc * np.float32(2.0 ** -0.5)))
        if has_rms:
            acc = acc * lax.rsqrt(jnp.mean(acc * acc, axis=-1, keepdims=True) + EPS) * g_ref[...]
        if has_tab:
            acc = acc * t_ref[...]
        o_ref[:, sl] = acc.astype(o_ref.dtype)


def mm(x, w, l, *, bias=None, x_act=None, act=None, rms_g=None, tab=None, tab_idx=None,
       out_dtype=BF, tm, tn, sub=256):
    m, k = x.shape
    n = w.shape[2]
    sub = min(sub, tn)
    if rms_g is not None:
        assert tn == n
        sub = tn
    in_specs = [pl.BlockSpec((tm, k), lambda j, i: (i, 0)),
                pl.BlockSpec((None, k, tn), lambda j, i: (l, 0, j))]
    args = [x, w]
    for v in (bias, rms_g):
        if v is not None:
            in_specs.append(pl.BlockSpec((None, 1, tn), lambda j, i: (l, 0, j)))
            args.append(v.reshape(v.shape[0], 1, n))
    if tab is not None:
        assert tab.shape[1] == sub
        in_specs.append(pl.BlockSpec((tm, sub), lambda j, i: (tab_idx(i), 0)))
        args.append(tab)
    kern = functools.partial(_mm_kernel, x_act=x_act, has_bias=bias is not None, act=act,
                             has_rms=rms_g is not None, has_tab=tab is not None, sub=sub)
    return pl.pallas_call(
        kern,
        grid=(n // tn, m // tm),
        in_specs=in_specs,
        out_specs=pl.BlockSpec((tm, tn), lambda j, i: (i, j)),
        out_shape=jax.ShapeDtypeStruct((m, n), out_dtype),
        scratch_shapes=[pltpu.VMEM((k, tn), BF)],
        compiler_params=_cparams(2),
        name="mm",
    )(*args)


def _gate_kernel(u_ref, v_ref, lg_ref, lb_ref, w_ref, bt_ref, *out_refs, block_diag, write_v):
    o_ref = out_refs[0]
    v = v_ref[...].astype(F32)
    mu = jnp.mean(v, axis=-1, keepdims=True)
    dv = v - mu
    var = jnp.mean(dv * dv, axis=-1, keepdims=True)
    vn = dv * lax.rsqrt(var + EPS) * lg_ref[...] + lb_ref[...]
    if write_v:
        out_refs[1][...] = vn
    vb = vn.astype(BF)
    row = lax.broadcasted_iota(jnp.int32, (CHUNK, CHUNK), 0)
    col = lax.broadcasted_iota(jnp.int32, (CHUNK, CHUNK), 1)
    mask = row >= col
    if block_diag:
        mask = jnp.logical_and(mask, (row // block_diag) == (col // block_diag))
    for g in range(GMLP_GROUPS):
        sl = slice(g * GROUP_DIM, (g + 1) * GROUP_DIM)
        wm = jnp.where(mask, w_ref[g], 0.0).astype(BF)
        s = jnp.dot(wm, vb[:, sl], preferred_element_type=F32) + bt_ref[:, g:g + 1]
        o_ref[:, sl] = (u_ref[:, sl].astype(F32) * s).astype(o_ref.dtype)


def spatial_gate(z, ln_g, ln_b, l, w_mix, b_t, *, block_diag, write_v):
    r = z.shape[0]
    nb = D_GMLP // D_GMLP
    out_shape = [jax.ShapeDtypeStruct((r, D_GMLP), BF)]
    out_specs = [pl.BlockSpec((CHUNK, D_GMLP), lambda c: (c, 0))]
    if write_v:
        out_shape.append(jax.ShapeDtypeStruct((r, D_GMLP), F32))
        out_specs.append(pl.BlockSpec((CHUNK, D_GMLP), lambda c: (c, 0)))
    res = pl.pallas_call(
        functools.partial(_gate_kernel, block_diag=block_diag, write_v=write_v),
        grid=(r // CHUNK,),
        in_specs=[pl.BlockSpec((CHUNK, D_GMLP), lambda c: (c, 0)),
                  pl.BlockSpec((CHUNK, D_GMLP), lambda c: (c, nb)),
                  pl.BlockSpec((None, 1, D_GMLP), lambda c: (l, 0, 0)),
                  pl.BlockSpec((None, 1, D_GMLP), lambda c: (l, 0, 0)),
                  pl.BlockSpec((GMLP_GROUPS, CHUNK, CHUNK), lambda c: (0, 0, 0)),
                  pl.BlockSpec((CHUNK, GMLP_GROUPS), lambda c: (0, 0))],
        out_specs=out_specs,
        out_shape=out_shape,
        compiler_params=_cparams(1),
        name="spatial_gate",
    )(z, z, ln_g.reshape(-1, 1, D_GMLP), ln_b.reshape(-1, 1, D_GMLP), w_mix, b_t)
    return res if write_v else (res[0], None)


def _down_kernel(*refs, conv_period, n_mod, nk, kc):
    it = iter(refs)
    if conv_period:
        g_ref, v_ref, p2_ref, cw_ref, cb_ref = (next(it) for _ in range(5))
    else:
        a_ref = next(it)
    w_ref, res_ref, gate_ref, pg_ref, pb_ref = (next(it) for _ in range(5))
    mod_refs = [(next(it), next(it)) for _ in range(n_mod)]
    xo_ref = next(it)
    h_refs = [next(it) for _ in range(n_mod)]
    tm = xo_ref.shape[0]
    tk = w_ref.shape[0]

    def conv_act(sl):
        g = g_ref[:, sl].astype(F32)
        p2 = p2_ref[:, sl]
        pr = p2.shape[0]
        p1 = pltpu.roll(p2, pr - 1, 0)
        r2, r1 = pltpu.roll(g, 2, 0), pltpu.roll(g, 1, 0)
        t = lax.broadcasted_iota(jnp.int32, p2.shape, 0) & (conv_period - 1)
        if pr == tm:
            gm2, gm1 = jnp.where(t < 2, p2, r2), jnp.where(t < 1, p1, r1)
        else:
            gm2 = jnp.concatenate([jnp.where(t < 2, p2, r2[:pr]), r2[pr:]], axis=0)
            gm1 = jnp.concatenate([jnp.where(t < 1, p1, r1[:pr]), r1[pr:]], axis=0)
        gc = cb_ref[:, sl] + cw_ref[0:1, sl] * gm2 + cw_ref[1:2, sl] * gm1 + cw_ref[2:3, sl] * g
        return (gc * jax.nn.sigmoid(gc) * v_ref[:, sl].astype(F32)).astype(BF)

    if conv_period:
        d = None
        for c in range(tk // kc):
            sl = slice(c * kc, (c + 1) * kc)
            dc = jnp.dot(conv_act(sl), w_ref[sl, :], preferred_element_type=F32)
            d = dc if d is None else d + dc
    else:
        d = jnp.dot(a_ref[...], w_ref[...], preferred_element_type=F32)

    def epilogue(rc, acc_rows):
        def rows(ref, r0):
            return ref[pl.ds(r0, rc), :] if ref.shape[0] == tm else ref[...]

        def chunk(c, carry):
            r0 = pl.multiple_of(c * rc, rc)
            y = ALPHA * res_ref[pl.ds(r0, rc), :] + rows(gate_ref, r0) * acc_rows(r0)
            mu = jnp.mean(y, axis=-1, keepdims=True)
            dy = y - mu
            var = jnp.mean(dy * dy, axis=-1, keepdims=True)
            yn = dy * lax.rsqrt(var + EPS) * pg_ref[...] + pb_ref[...]
            xo_ref[pl.ds(r0, rc), :] = yn
            for (sc_ref, sh_ref), h_ref in zip(mod_refs, h_refs):
                h_ref[pl.ds(r0, rc), :] = (yn * (1.0 + rows(sc_ref, r0)) + rows(sh_ref, r0)).astype(h_ref.dtype)
            return carry

        lax.fori_loop(0, tm // rc, chunk, 0)

    if nk == 1:
        epilogue(tm, lambda r0: d)
    else:
        k = pl.program_id(1)

        @pl.when(k == 0)
        def _():
            xo_ref[...] = d

        @pl.when(k > 0)
        def _():
            xo_ref[...] += d

        @pl.when(k == nk - 1)
        def _():
            rc = min(tm, LN_ROWS)
            epilogue(rc, lambda r0: xo_ref[pl.ds(r0, rc), :])


def mm_down_ln(a, w, l, resid, gate, post_g, post_b, ln_idx, mods, *, conv=None, tm, tk=None):
    r = resid.shape[0]
    k_dim, n = w.shape[1], w.shape[2]
    tk = k_dim if tk is None else tk
    nk = k_dim // tk
    row = lambda i, k: i
    in_specs, args = [], []
    if conv is not None:
        gv, p2, cw, cb, period = conv
        pr = p2.shape[0] // (r // tm)
        in_specs += [pl.BlockSpec((tm, tk), lambda i, k: (i, k)),
                     pl.BlockSpec((tm, tk), lambda i, k: (i, k + nk)),
                     pl.BlockSpec((pr, tk), lambda i, k: (i, k)),
                     pl.BlockSpec((None, CONV_W, tk), lambda i, k: (l, 0, k)),
                     pl.BlockSpec((None, 1, tk), lambda i, k: (l, 0, k))]
        args += [gv, gv, p2, cw, cb.reshape(cb.shape[0], 1, k_dim)]
        conv_period = tm if period is None else period
    else:
        in_specs.append(pl.BlockSpec((tm, tk), lambda i, k: (i, k)))
        args.append(a)
        conv_period = 0
    w_mode = dict(pipeline_mode=pl.Buffered(1)) if nk == 1 else {}
    in_specs += [pl.BlockSpec((None, tk, n), lambda i, k: (l, k, 0), **w_mode),
                 pl.BlockSpec((tm, n), lambda i, k: (i, 0)),
                 gate.spec(tm, row),
                 pl.BlockSpec((None, 1, n), lambda i, k: (ln_idx, 0, 0)),
                 pl.BlockSpec((None, 1, n), lambda i, k: (ln_idx, 0, 0))]
    args += [w, resid, gate.arr, post_g.reshape(-1, 1, n), post_b.reshape(-1, 1, n)]
    for sc, sh in mods:
        in_specs += [sc.spec(tm, row), sh.spec(tm, row)]
        args += [sc.arr, sh.arr]
    out_shape = [jax.ShapeDtypeStruct((r, n), F32)] + [jax.ShapeDtypeStruct((r, n), BF)] * len(mods)
    out_specs = [pl.BlockSpec((tm, n), lambda i, k: (i, 0)) for _ in out_shape]
    res = pl.pallas_call(
        functools.partial(_down_kernel, conv_period=conv_period, n_mod=len(mods), nk=nk, kc=min(tk, CONV_COLS)),
        grid=(r // tm, nk),
        in_specs=in_specs,
        out_specs=out_specs,
        out_shape=out_shape,
        compiler_params=_cparams(2),
        name="mm_down_ln",
    )(*args)
    return res[0], list(res[1:])


def _dkv_kernel(x_ref, w_ref, g_ref, cs_ref, lat_ref, kr_ref):
    a = jnp.dot(x_ref[...], w_ref[...].astype(BF), preferred_element_type=F32)
    lat = a[:, :KV_LORA]
    lat_ref[...] = lat * lax.rsqrt(jnp.mean(lat * lat, axis=-1, keepdims=True) + EPS) * g_ref[...]
    t = a[:, KV_LORA:] * cs_ref[...]
    kr_ref[...] = t[:, :QK_ROPE] + t[:, QK_ROPE:]


def dkv(x, w_ext, norm_g, cs, cs_idx, tm):
    r, k = x.shape
    n = w_ext.shape[1]
    return pl.pallas_call(
        _dkv_kernel,
        grid=(r // tm,),
        in_specs=[pl.BlockSpec((tm, k), lambda i: (i, 0)),
                  pl.BlockSpec((k, n), lambda i: (0, 0)),
                  pl.BlockSpec((1, KV_LORA), lambda i: (0, 0)),
                  pl.BlockSpec((tm, 2 * QK_ROPE), lambda i: (cs_idx(i), 0))],
        out_specs=[pl.BlockSpec((tm, KV_LORA), lambda i: (i, 0)),
                   pl.BlockSpec((tm, QK_ROPE), lambda i: (i, 0))],
        out_shape=[jax.ShapeDtypeStruct((r, KV_LORA), F32), jax.ShapeDtypeStruct((r, QK_ROPE), F32)],
        compiler_params=_cparams(1),
        name="dkv",
    )(x, w_ext, norm_g.reshape(1, KV_LORA), cs)


def _flash_kernel(q_ref, kn_ref, kr_ref, v_ref, o_ref, *, tq):
    qi = pl.program_id(2)
    q = q_ref[...]

    def step(j, carry, masked):
        m, l, acc = carry
        ks = pl.multiple_of(j * tq, tq)
        kk = jnp.concatenate([kn_ref[pl.ds(ks, tq), :], kr_ref[pl.ds(ks, tq), :]], axis=1)
        s = lax.dot_general(q, kk, (((1,), (1,)), ((), ())), preferred_element_type=F32)
        if masked:
            row = lax.broadcasted_iota(jnp.int32, (tq, tq), 0)
            col = lax.broadcasted_iota(jnp.int32, (tq, tq), 1)
            s = jnp.where(row >= col, s, -jnp.inf)
        m_new = jnp.maximum(m, jnp.max(s, axis=-1, keepdims=True))
        alpha = jnp.exp2(m - m_new)
        p = jnp.exp2(s - m_new)
        l = alpha * l + jnp.sum(p, axis=-1, keepdims=True)
        acc = alpha * acc + jnp.dot(p.astype(BF), v_ref[pl.ds(ks, tq), :], preferred_element_type=F32)
        return m_new, l, acc

    init = (jnp.full((tq, 1), -jnp.inf, F32), jnp.zeros((tq, 1), F32), jnp.zeros((tq, V_HEAD), F32))
    carry = lax.fori_loop(0, qi, lambda j, c: step(j, c, False), init)
    _, l, acc = step(qi, carry, True)
    o_ref[...] = (acc / l).astype(o_ref.dtype)


def flash_prompt(q, kn, kr2, v, bsz, seq, tq):
    r = q.shape[0]
    nq = seq // tq
    return pl.pallas_call(
        functools.partial(_flash_kernel, tq=tq),
        grid=(bsz, N_HEADS, nq),
        in_specs=[pl.BlockSpec((tq, HEAD_W), lambda b, h, i: (b * nq + i, h)),
                  pl.BlockSpec((seq, QK_NOPE), lambda b, h, i: (b, h)),
                  pl.BlockSpec((seq, 2 * QK_ROPE), lambda b, h, i: (b, 0)),
                  pl.BlockSpec((seq, V_HEAD), lambda b, h, i: (b, h))],
        out_specs=pl.BlockSpec((tq, V_HEAD), lambda b, h, i: (b * nq + i, h)),
        out_shape=jax.ShapeDtypeStruct((r, N_HEADS * V_HEAD), BF),
        compiler_params=_cparams(3),
        name="flash_prompt",
    )(q, kn, kr2, v)


def _qlat_kernel(x_ref, w_ref, o_ref):
    x = x_ref[...]
    o_ref[:, :KV_LORA] = jnp.dot(x[:, :QK_NOPE], w_ref[...].astype(BF), preferred_element_type=F32)
    o_ref[:, KV_LORA:] = x[:, QK_NOPE:].astype(F32)


def q_to_latent(q, w_uk_t):
    r = q.shape[0]
    return pl.pallas_call(
        _qlat_kernel,
        grid=(N_HEADS,),
        in_specs=[pl.BlockSpec((r, HEAD_W), lambda h: (0, h)),
                  pl.BlockSpec((None, QK_NOPE, KV_LORA), lambda h: (h, 0, 0))],
        out_specs=pl.BlockSpec((None, r, KV_LORA + 2 * QK_ROPE), lambda h: (h, 0, 0)),
        out_shape=jax.ShapeDtypeStruct((N_HEADS, r, KV_LORA + 2 * QK_ROPE), F32),
        compiler_params=_cparams(1),
        name="q_to_latent",
    )(q, w_uk_t)


def _paged_kernel(pt_ref, q_ref, *refs, n_seq, n_pages, n_steps, t_new):
    n_in = n_seq * n_pages
    lat_refs = refs[:n_in]
    kr_refs = refs[n_in:2 * n_in]
    latn_ref, krn_ref, o_ref, ql_s, qr_s, m_s, l_s, acc_s = refs[2 * n_in:]
    jg = pl.program_id(1)
    rows = N_HEADS * t_new
    contract_last = (((1,), (1,)), ((), ()))

    @pl.when(jg == 0)
    def _():
        for sq in range(n_seq):
            q = q_ref[:, sq * t_new:(sq + 1) * t_new, :].reshape(rows, KV_LORA + 2 * QK_ROPE)
            ql_s[sq] = q[:, :KV_LORA].astype(BF)
            qr_s[sq] = (q[:, KV_LORA:KV_LORA + QK_ROPE] + q[:, KV_LORA + QK_ROPE:]).astype(BF)
            m_s[sq] = jnp.full((rows, 1), -jnp.inf, F32)
            l_s[sq] = jnp.zeros((rows, 1), F32)
            acc_s[sq] = jnp.zeros((rows, KV_LORA), F32)

    def update(sq, lat, s_rope, mask):
        s = lax.dot_general(ql_s[sq], lat, contract_last, preferred_element_type=F32) + s_rope
        if mask is not None:
            s = jnp.where(mask, s, -jnp.inf)
        m_old = m_s[sq]
        m_new = jnp.maximum(m_old, jnp.max(s, axis=-1, keepdims=True))
        alpha = jnp.exp2(m_old - m_new)
        p = jnp.exp2(s - m_new)
        l_s[sq] = alpha * l_s[sq] + jnp.sum(p, axis=-1, keepdims=True)
        acc_s[sq] = alpha * acc_s[sq] + jnp.dot(p.astype(BF), lat, preferred_element_type=F32)
        m_s[sq] = m_new

    for sq in range(n_seq):
        pages = slice(sq * n_pages, (sq + 1) * n_pages)
        lat = jnp.concatenate([r[...].astype(BF) for r in lat_refs[pages]], axis=0)
        kr_t = jnp.concatenate([r[...].astype(BF) for r in kr_refs[pages]], axis=1)
        update(sq, lat, jnp.dot(qr_s[sq], kr_t, preferred_element_type=F32), None)

    @pl.when(jg == n_steps - 1)
    def _():
        n_pad = latn_ref.shape[1]
        tq = lax.broadcasted_iota(jnp.int32, (rows, n_pad), 0) & (t_new - 1)
        tk = lax.broadcasted_iota(jnp.int32, (rows, n_pad), 1)
        for sq in range(n_seq):
            s_rope = lax.dot_general(qr_s[sq], krn_ref[sq], contract_last, preferred_element_type=F32)
            update(sq, latn_ref[sq], s_rope, tk <= tq)
            o_ref[:, sq * t_new:(sq + 1) * t_new, :] = (acc_s[sq] / l_s[sq]).reshape(N_HEADS, t_new, KV_LORA)


def paged_attention(q_lat, cache_lat, cache_kr_t, page_table, lat_new, kr_new, t_new):
    bsz, n_tbl = page_table.shape
    page = cache_lat.shape[1]
    npg, nsq = PAGES_PER_STEP, SEQS_PER_STEP
    n_steps = n_tbl // npg
    rows = N_HEADS * t_new
    n_pad = lat_new.shape[1]
    qw = KV_LORA + 2 * QK_ROPE

    def page_spec(shape, sq, p):
        return pl.BlockSpec((None,) + shape,
                            lambda b, j, pt: (pt[(b * nsq + sq) * n_tbl + j * npg + p], 0, 0))

    in_specs = [pl.BlockSpec((N_HEADS, nsq * t_new, qw), lambda b, j, pt: (0, b, 0))]
    in_specs += [page_spec((page, KV_LORA), sq, p) for sq in range(nsq) for p in range(npg)]
    in_specs += [page_spec((QK_ROPE, page), sq, p) for sq in range(nsq) for p in range(npg)]
    in_specs += [pl.BlockSpec((nsq, n_pad, KV_LORA), lambda b, j, pt: (b, 0, 0)),
                 pl.BlockSpec((nsq, n_pad, QK_ROPE), lambda b, j, pt: (b, 0, 0))]
    grid_spec = pltpu.PrefetchScalarGridSpec(
        num_scalar_prefetch=1,
        grid=(bsz // nsq, n_steps),
        in_specs=in_specs,
        out_specs=pl.BlockSpec((N_HEADS, nsq * t_new, KV_LORA), lambda b, j, pt: (0, b, 0)),
        scratch_shapes=[pltpu.VMEM((nsq, rows, KV_LORA), BF), pltpu.VMEM((nsq, rows, QK_ROPE), BF),
                        pltpu.VMEM((nsq, rows, 1), F32), pltpu.VMEM((nsq, rows, 1), F32),
                        pltpu.VMEM((nsq, rows, KV_LORA), F32)])
    n_in = nsq * npg
    return pl.pallas_call(
        functools.partial(_paged_kernel, n_seq=nsq, n_pages=npg, n_steps=n_steps, t_new=t_new),
        grid_spec=grid_spec,
        out_shape=jax.ShapeDtypeStruct((N_HEADS, bsz * t_new, KV_LORA), F32),
        compiler_params=_cparams(2),
        name="paged_attention",
    )(page_table.reshape(-1), q_lat, *([cache_lat] * n_in), *([cache_kr_t] * n_in), lat_new, kr_new)


def _ov_kernel(o_ref, w_ref, out_ref):
    out_ref[...] = jnp.dot(o_ref[...].astype(BF), w_ref[...].astype(BF),
                           preferred_element_type=F32).astype(out_ref.dtype)


def latent_to_v(o_lat, w_uv2d):
    r = o_lat.shape[1]
    return pl.pallas_call(
        _ov_kernel,
        grid=(N_HEADS,),
        in_specs=[pl.BlockSpec((None, r, KV_LORA), lambda h: (h, 0, 0)),
                  pl.BlockSpec((KV_LORA, V_HEAD), lambda h: (0, h))],
        out_specs=pl.BlockSpec((r, V_HEAD), lambda h: (0, h)),
        out_shape=jax.ShapeDtypeStruct((r, N_HEADS * V_HEAD), BF),
        compiler_params=_cparams(1),
        name="latent_to_v",
    )(o_lat, w_uv2d)


def _rope_swap(w):
    half = QK_ROPE // 2
    return jnp.concatenate([-w[..., half:], w[..., :half]], axis=-1)


def _rope_tables(pos):
    half = QK_ROPE // 2
    inv = jnp.power(ROPE_THETA, -jnp.arange(half, dtype=F32) / half)
    ang = pos.astype(F32)[:, None] * inv[None, :]
    cos, sin = jnp.cos(ang), jnp.sin(ang)
    cs = jnp.concatenate([cos, cos, sin, sin], axis=-1)
    q_tab = (SM_SCALE * LOG2E) * jnp.concatenate([jnp.ones((pos.shape[0], QK_NOPE), F32), cs], axis=-1)
    return cs, q_tab


def _trunk(x3, c_mods, kv_mod, pos, conv_state, paged, W, tm):
    bsz, t_rows, d = x3.shape
    r = bsz * t_rows
    x = x3.reshape(r, d)
    is_prompt = paged is None
    tiles_per_seq = max(t_rows // tm, 1)
    cs, q_tab = _rope_tables(pos)
    if is_prompt:
        tab_idx = lambda i: i % tiles_per_seq
    else:
        cs, q_tab = jnp.tile(cs, (tm // t_rows, 1)), jnp.tile(q_tab, (tm // t_rows, 1))
        tab_idx = lambda i: 0

    def vecs(mat, n):
        return [RowVec(mat[:, i * d:(i + 1) * d], t_rows) for i in range(n)]

    layer_mods = [vecs(m, 6) for m in c_mods]
    sc_kv_sh = vecs(kv_mod, 2)

    conv_new, v_rows = [], []
    latent = k_rope = None
    kv_ctx = None
    tm_d = 256 if is_prompt else 128
    tk_d = None
    h = modulate(x, layer_mods[0][1], layer_mods[0][0], 512 if is_prompt else 256)
    for l in range(DEPTH):
        sh1, sc1, g1, sh2, sc2, g2 = layer_mods[l]
        if l < N_A_LAYERS:
            z = mm(h, W["gm_w_in"], l, bias=W["gm_b_in"], act="gelu", tm=tm, tn=1024)
            if is_prompt:
                w_mix, b_t, bd = W["gm_w_s"][l], W["gm_b_s"][l].T, 0
            else:
                reps = CHUNK // t_rows
                w_mix = jnp.tile(W["gm_w_s"][l][:, :t_rows, :t_rows], (1, reps, reps))
                b_t = jnp.tile(W["gm_b_s"][l].T[:t_rows], (reps, 1))
                bd = t_rows
            gated, vn = spatial_gate(z, W["gm_ln_g"], W["gm_ln_b"], l, w_mix, b_t,
                                     block_diag=bd, write_v=not is_prompt)
            if vn is not None:
                v_rows.append(vn.reshape(bsz, t_rows, D_GMLP))
            x, (h,) = mm_down_ln(gated, W["gm_w_out"], l, x, g1, W["post_ln_g"], W["post_ln_b"],
                                 2 * l, [(sc2, sh2)], tm=tm_d, tk=tk_d)
        else:
            j = l - N_A_LAYERS
            cq = mm(h, W["w_dq"], j, rms_g=W["q_norm_g"], tm=tm, tn=Q_LORA)
            q = mm(cq, W["w_uq_ext"], j, tab=q_tab, tab_idx=tab_idx, tm=tm, tn=1024, sub=HEAD_W)
            if is_prompt:
                kn, kr2, v = kv_ctx
                o = flash_prompt(q, kn, kr2, v, bsz, t_rows, 1024)
            else:
                q_lat = q_to_latent(q, W["w_uk_t"])
                lat_new, kr_new = kv_ctx
                o_lat = paged_attention(q_lat, paged[0], paged[1], paged[2], lat_new, kr_new, t_rows)
                o = latent_to_v(o_lat, W["w_uv2d"])
            x, (h,) = mm_down_ln(o, W["w_o2d"], j, x, g1, W["post_ln_g"], W["post_ln_b"],
                                 2 * l, [(sc2, sh2)], tm=tm_d, tk=tk_d)
        gv = mm(h, W["ffn_w_up"], l, tm=tm, tn=1024)
        conv_new.append(gv.reshape(bsz, t_rows, 2 * D_FF)[:, t_rows - (CONV_W - 1):, :D_FF].astype(F32))
        mods = []
        if l + 1 < DEPTH:
            nsh1, nsc1 = layer_mods[l + 1][0], layer_mods[l + 1][1]
            mods.append((nsc1, nsh1))
        if l == N_A_LAYERS - 1:
            mods.append((sc_kv_sh[1], sc_kv_sh[0]))
        tm_f = tm_d
        if is_prompt:
            n_tiles = r // tm_f
            last2 = gv.reshape(n_tiles, tm_f, 2 * D_FF)[:, tm_f - 2:, :D_FF].astype(F32)
            prev2 = jnp.concatenate([jnp.zeros((1, 2, D_FF), F32), last2[:-1]], axis=0)
            starts = (jnp.arange(n_tiles) * tm_f) % t_rows == 0
            prev2 = jnp.where(starts[:, None, None], 0.0, prev2)
            period = None
        else:
            prev2 = conv_state[l].astype(F32)
            period = t_rows
        p2 = jnp.pad(prev2, ((0, 0), (0, 6), (0, 0))).reshape(-1, D_FF)
        x, hs = mm_down_ln(None, W["ffn_w_down"], l, x, g2, W["post_ln_g"], W["post_ln_b"], 2 * l + 1, mods,
                           conv=(gv, p2, W["ffn_conv_w"], W["ffn_conv_b"], period), tm=tm_f, tk=tk_d)
        if hs:
            h = hs[0]
        if l == N_A_LAYERS - 1:
            latent, k_rope = dkv(hs[1], W["w_dkv_ext"], W["kv_norm_g"], cs, tab_idx, tm)
            lat_bf = latent.astype(BF)
            if is_prompt:
                kn = mm(lat_bf, W["w_uk2d"][None], 0, tm=tm, tn=1024)
                v = mm(lat_bf, W["w_uv2d"][None], 0, tm=tm, tn=1024)
                kr2 = jnp.concatenate([k_rope, k_rope], axis=-1).astype(BF)
                kv_ctx = (kn, kr2, v)
            else:
                pad = ((0, 0), (0, 16 - t_rows), (0, 0))
                kv_ctx = (jnp.pad(lat_bf.reshape(bsz, t_rows, KV_LORA), pad),
                          jnp.pad(k_rope.astype(BF).reshape(bsz, t_rows, QK_ROPE), pad))
    y = x.reshape(bsz, t_rows, d)
    return (y, latent.reshape(bsz, t_rows, KV_LORA), k_rope.reshape(bsz, t_rows, QK_ROPE),
            jnp.stack(conv_new), jnp.stack(v_rows) if v_rows else None)


def kernel(x_prompt, x_sample, cache_kv_latent, cache_k_rope, state_conv, page_table, c_prompt, c_sample,
           ada_w, ada_b, post_ln_g, post_ln_b,
           gm_w_in, gm_b_in, gm_ln_g, gm_ln_b, gm_w_s, gm_b_s, gm_w_out,
           kv_ada_w, kv_ada_b, w_dkv, kv_norm_g, w_uk, w_uv,
           w_dq, q_norm_g, w_uq, w_o,
           ffn_w_up, ffn_conv_w, ffn_conv_b, ffn_w_down):
    d = D_MODEL
    n_b = w_dq.shape[0]
    w_uq_rope = w_uq[..., QK_NOPE:]
    w_uq_ext = jnp.concatenate([w_uq[..., :QK_NOPE], w_uq_rope, _rope_swap(w_uq_rope)], axis=-1)
    w_dkv_rope = w_dkv[:, KV_LORA:]
    W = dict(
        gm_w_in=gm_w_in, gm_b_in=gm_b_in, gm_ln_g=gm_ln_g, gm_ln_b=gm_ln_b, gm_w_s=gm_w_s, gm_b_s=gm_b_s,
        gm_w_out=gm_w_out.astype(BF), post_ln_g=post_ln_g.reshape(2 * DEPTH, d), post_ln_b=post_ln_b.reshape(2 * DEPTH, d),
        w_dq=w_dq, q_norm_g=q_norm_g,
        w_uq_ext=w_uq_ext.reshape(n_b, Q_LORA, N_HEADS * HEAD_W),
        w_o2d=w_o.reshape(n_b, N_HEADS * V_HEAD, d).astype(BF),
        w_dkv_ext=jnp.concatenate([w_dkv, _rope_swap(w_dkv_rope)], axis=-1),
        kv_norm_g=kv_norm_g,
        w_uk2d=w_uk.reshape(KV_LORA, N_HEADS * QK_NOPE),
        w_uv2d=w_uv.reshape(KV_LORA, N_HEADS * V_HEAD),
        w_uk_t=jnp.transpose(w_uk, (1, 2, 0)),
        ffn_w_up=ffn_w_up, ffn_conv_w=ffn_conv_w, ffn_conv_b=ffn_conv_b, ffn_w_down=ffn_w_down.astype(BF),
    )

    n_p, n_s = c_prompt.shape[0], c_sample.shape[0]
    n_c = -(-(n_p + n_s) // 8) * 8
    c_all = jnp.pad(jnp.concatenate([c_prompt, c_sample], axis=0), ((0, n_c - n_p - n_s), (0, 0)))
    mods = [mm(c_all, ada_w, l, bias=ada_b, x_act="silu", out_dtype=F32, tm=n_c, tn=1024) for l in range(DEPTH)]
    kv_mod = mm(c_all, kv_ada_w[None], 0, bias=kv_ada_b[None], x_act="silu", out_dtype=F32, tm=n_c, tn=1024)

    seq_p = x_prompt.shape[1]
    y_p, lat_p, kr_p, conv_p, _ = _trunk(
        x_prompt, [m[:n_p] for m in mods], kv_mod[:n_p], jnp.arange(seq_p, dtype=jnp.int32),
        None, None, W, tm=2048)

    past = page_table.shape[1] * cache_kv_latent.shape[1]
    seq_s = x_sample.shape[1]
    y_s, lat_s, kr_s, conv_s, v_s = _trunk(
        x_sample, [m[n_p:n_p + n_s] for m in mods], kv_mod[n_p:n_p + n_s],
        past + jnp.arange(seq_s, dtype=jnp.int32), state_conv,
        (cache_kv_latent, jnp.swapaxes(cache_k_rope, 1, 2), page_table), W, tm=1024)

    return (y_p, y_s, lat_p, kr_p, lat_s, kr_s, conv_p, conv_s, v_s)
```

```python
import functools

import numpy as np
import jax
import jax.numpy as jnp
from jax import lax
from jax.experimental import pallas as pl
from jax.experimental.pallas import tpu as pltpu

F32 = jnp.float32
BF = jnp.bfloat16

D_MODEL = 2048
DEPTH = 4
N_A_LAYERS = DEPTH // 2
CHUNK = 128
GMLP_GROUPS = 16
D_GMLP = 3 * D_MODEL
GROUP_DIM = D_GMLP // GMLP_GROUPS
N_HEADS = 16
QK_NOPE = 128
QK_ROPE = 64
V_HEAD = 128
KV_LORA = 512
Q_LORA = 512
ROPE_THETA = 10000.0
SM_SCALE = (QK_NOPE + QK_ROPE) ** -0.5
D_FF = 5632
CONV_W = 3
ALPHA = (2 * DEPTH) ** 0.25
EPS = 1e-5
HEAD_W = 2 * QK_NOPE
PAGES_PER_STEP = 64
SEQS_PER_STEP = 1
CONV_COLS = 256
LN_ROWS = 256
LOG2E = 1.4426950408889634
VMEM_LIMIT = 60000 * 1024


def _cparams(n_axes):
    return pltpu.CompilerParams(dimension_semantics=("arbitrary",) * n_axes,
                                vmem_limit_bytes=VMEM_LIMIT)


class RowVec:
    def __init__(self, v, t_rows):
        self.b, self.d = v.shape
        self.t = t_rows
        self.expanded = t_rows < CHUNK
        self.arr = jnp.repeat(v, t_rows, axis=0) if self.expanded else v.reshape(self.b, 1, self.d)

    def spec(self, tm, row_tile):
        if self.expanded:
            return pl.BlockSpec((tm, self.d), lambda *g: (row_tile(*g), 0))
        t = self.t
        return pl.BlockSpec((None, 1, self.d), lambda *g: ((row_tile(*g) * tm) // t, 0, 0))


def _modulate_kernel(x_ref, sc_ref, sh_ref, o_ref):
    o_ref[...] = (x_ref[...] * (1.0 + sc_ref[...]) + sh_ref[...]).astype(o_ref.dtype)


def modulate(x, sc, sh, tm):
    r, d = x.shape
    row = lambda i: i
    return pl.pallas_call(
        _modulate_kernel,
        grid=(r // tm,),
        in_specs=[pl.BlockSpec((tm, d), lambda i: (i, 0)), sc.spec(tm, row), sh.spec(tm, row)],
        out_specs=pl.BlockSpec((tm, d), lambda i: (i, 0)),
        out_shape=jax.ShapeDtypeStruct((r, d), BF),
        compiler_params=_cparams(1),
        name="modulate",
    )(x, sc.arr, sh.arr)


def _mm_kernel(*refs, x_act, has_bias, act, has_rms, has_tab, sub):
    it = iter(refs)
    x_ref, w_ref = next(it), next(it)
    b_ref = next(it) if has_bias else None
    g_ref = next(it) if has_rms else None
    t_ref = next(it) if has_tab else None
    o_ref, wb = next(it), next(it)

    @pl.when(pl.program_id(1) == 0)
    def _():
        wb[...] = w_ref[...].astype(BF)

    xv = x_ref[...]
    if x_act == "silu":
        xf = xv.astype(F32)
        xv = xf * jax.nn.sigmoid(xf)
    xb = xv.astype(BF)
    tn = o_ref.shape[1]
    for c in range(tn // sub):
        sl = slice(c * sub, (c + 1) * sub)
        acc = jnp.dot(xb, wb[:, sl], preferred_element_type=F32)
        if has_bias:
            acc = acc + b_ref[:, sl]
        if act == "gelu":
            acc = 0.5 * acc * (1.0 + lax.erf(acc * np.float32(2.0 ** -0.5)))
        if has_rms:
            acc = acc * lax.rsqrt(jnp.mean(acc * acc, axis=-1, keepdims=True) + EPS) * g_ref[...]
        if has_tab:
            acc = acc * t_ref[...]
        o_ref[:, sl] = acc.astype(o_ref.dtype)


def mm(x, w, l, *, bias=None, x_act=None, act=None, rms_g=None, tab=None, tab_idx=None,
       out_dtype=BF, tm, tn, sub=256):
    m, k = x.shape
    n = w.shape[2]
    sub = min(sub, tn)
    if rms_g is not None:
        assert tn == n
        sub = tn
    in_specs = [pl.BlockSpec((tm, k), lambda j, i: (i, 0)),
                pl.BlockSpec((None, k, tn), lambda j, i: (l, 0, j))]
    args = [x, w]
    for v in (bias, rms_g):
        if v is not None:
            in_specs.append(pl.BlockSpec((None, 1, tn), lambda j, i: (l, 0, j)))
            args.append(v.reshape(v.shape[0], 1, n))
    if tab is not None:
        assert tab.shape[1] == sub
        in_specs.append(pl.BlockSpec((tm, sub), lambda j, i: (tab_idx(i), 0)))
        args.append(tab)
    kern = functools.partial(_mm_kernel, x_act=x_act, has_bias=bias is not None, act=act,
                             has_rms=rms_g is not None, has_tab=tab is not None, sub=sub)
    return pl.pallas_call(
        kern,
        grid=(n // tn, m // tm),
        in_specs=in_specs,
        out_specs=pl.BlockSpec((tm, tn), lambda j, i: (i, j)),
        out_shape=jax.ShapeDtypeStruct((m, n), out_dtype),
        scratch_shapes=[pltpu.VMEM((k, tn), BF)],
        compiler_params=_cparams(2),
        name="mm",
    )(*args)


def _gate_kernel(u_ref, v_ref, lg_ref, lb_ref, w_ref, bt_ref, *out_refs, block_diag, write_v):
    o_ref = out_refs[0]
    v = v_ref[...].astype(F32)
    mu = jnp.mean(v, axis=-1, keepdims=True)
    dv = v - mu
    var = jnp.mean(dv * dv, axis=-1, keepdims=True)
    vn = dv * lax.rsqrt(var + EPS) * lg_ref[...] + lb_ref[...]
    if write_v:
        out_refs[1][...] = vn
    vb = vn.astype(BF)
    row = lax.broadcasted_iota(jnp.int32, (CHUNK, CHUNK), 0)
    col = lax.broadcasted_iota(jnp.int32, (CHUNK, CHUNK), 1)
    mask = row >= col
    if block_diag:
        mask = jnp.logical_and(mask, (row // block_diag) == (col // block_diag))
    for g in range(GMLP_GROUPS):
        sl = slice(g * GROUP_DIM, (g + 1) * GROUP_DIM)
        wm = jnp.where(mask, w_ref[g], 0.0).astype(BF)
        s = jnp.dot(wm, vb[:, sl], preferred_element_type=F32) + bt_ref[:, g:g + 1]
        o_ref[:, sl] = (u_ref[:, sl].astype(F32) * s).astype(o_ref.dtype)


def spatial_gate(z, ln_g, ln_b, l, w_mix, b_t, *, block_diag, write_v):
    r = z.shape[0]
    nb = D_GMLP // D_GMLP
    out_shape = [jax.ShapeDtypeStruct((r, D_GMLP), BF)]
    out_specs = [pl.BlockSpec((CHUNK, D_GMLP), lambda c: (c, 0))]
    if write_v:
        out_shape.append(jax.ShapeDtypeStruct((r, D_GMLP), F32))
        out_specs.append(pl.BlockSpec((CHUNK, D_GMLP), lambda c: (c, 0)))
    res = pl.pallas_call(
        functools.partial(_gate_kernel, block_diag=block_diag, write_v=write_v),
        grid=(r // CHUNK,),
        in_specs=[pl.BlockSpec((CHUNK, D_GMLP), lambda c: (c, 0)),
                  pl.BlockSpec((CHUNK, D_GMLP), lambda c: (c, nb)),
                  pl.BlockSpec((None, 1, D_GMLP), lambda c: (l, 0, 0)),
                  pl.BlockSpec((None, 1, D_GMLP), lambda c: (l, 0, 0)),
                  pl.BlockSpec((GMLP_GROUPS, CHUNK, CHUNK), lambda c: (0, 0, 0)),
                  pl.BlockSpec((CHUNK, GMLP_GROUPS), lambda c: (0, 0))],
        out_specs=out_specs,
        out_shape=out_shape,
        compiler_params=_cparams(1),
        name="spatial_gate",
    )(z, z, ln_g.reshape(-1, 1, D_GMLP), ln_b.reshape(-1, 1, D_GMLP), w_mix, b_t)
    return res if write_v else (res[0], None)


def _down_kernel(*refs, conv_period, n_mod, nk, kc):
    it = iter(refs)
    if conv_period:
        g_ref, v_ref, p2_ref, cw_ref, cb_ref = (next(it) for _ in range(5))
    else:
        a_ref = next(it)
    w_ref, res_ref, gate_ref, pg_ref, pb_ref = (next(it) for _ in range(5))
    mod_refs = [(next(it), next(it)) for _ in range(n_mod)]
    xo_ref = next(it)
    h_refs = [next(it) for _ in range(n_mod)]
    tm = xo_ref.shape[0]
    tk = w_ref.shape[0]

    def conv_act(sl):
        g = g_ref[:, sl].astype(F32)
        p2 = p2_ref[:, sl]
        pr = p2.shape[0]
        p1 = pltpu.roll(p2, pr - 1, 0)
        r2, r1 = pltpu.roll(g, 2, 0), pltpu.roll(g, 1, 0)
        t = lax.broadcasted_iota(jnp.int32, p2.shape, 0) & (conv_period - 1)
        if pr == tm:
            gm2, gm1 = jnp.where(t < 2, p2, r2), jnp.where(t < 1, p1, r1)
        else:
            gm2 = jnp.concatenate([jnp.where(t < 2, p2, r2[:pr]), r2[pr:]], axis=0)
            gm1 = jnp.concatenate([jnp.where(t < 1, p1, r1[:pr]), r1[pr:]], axis=0)
        gc = cb_ref[:, sl] + cw_ref[0:1, sl] * gm2 + cw_ref[1:2, sl] * gm1 + cw_ref[2:3, sl] * g
        return (gc * jax.nn.sigmoid(gc) * v_ref[:, sl].astype(F32)).astype(BF)

    if conv_period:
        d = None
        for c in range(tk // kc):
            sl = slice(c * kc, (c + 1) * kc)
            dc = jnp.dot(conv_act(sl), w_ref[sl, :], preferred_element_type=F32)
            d = dc if d is None else d + dc
    else:
        d = jnp.dot(a_ref[...], w_ref[...], preferred_element_type=F32)

    def epilogue(rc, acc_rows):
        def rows(ref, r0):
            return ref[pl.ds(r0, rc), :] if ref.shape[0] == tm else ref[...]

        def chunk(c, carry):
            r0 = pl.multiple_of(c * rc, rc)
            y = ALPHA * res_ref[pl.ds(r0, rc), :] + rows(gate_ref, r0) * acc_rows(r0)
            mu = jnp.mean(y, axis=-1, keepdims=True)
            dy = y - mu
            var = jnp.mean(dy * dy, axis=-1, keepdims=True)
            yn = dy * lax.rsqrt(var + EPS) * pg_ref[...] + pb_ref[...]
            xo_ref[pl.ds(r0, rc), :] = yn
            for (sc_ref, sh_ref), h_ref in zip(mod_refs, h_refs):
                h_ref[pl.ds(r0, rc), :] = (yn * (1.0 + rows(sc_ref, r0)) + rows(sh_ref, r0)).astype(h_ref.dtype)
            return carry

        lax.fori_loop(0, tm // rc, chunk, 0)

    if nk == 1:
        epilogue(tm, lambda r0: d)
    else:
        k = pl.program_id(1)

        @pl.when(k == 0)
        def _():
            xo_ref[...] = d

        @pl.when(k > 0)
        def _():
            xo_ref[...] += d

        @pl.when(k == nk - 1)
        def _():
            rc = min(tm, LN_ROWS)
            epilogue(rc, lambda r0: xo_ref[pl.ds(r0, rc), :])


def mm_down_ln(a, w, l, resid, gate, post_g, post_b, ln_idx, mods, *, conv=None, tm, tk=None):
    r = resid.shape[0]
    k_dim, n = w.shape[1], w.shape[2]
    tk = k_dim if tk is None else tk
    nk = k_dim // tk
    row = lambda i, k: i
    in_specs, args = [], []
    if conv is not None:
        gv, p2, cw, cb, period = conv
        pr = p2.shape[0] // (r // tm)
        in_specs += [pl.BlockSpec((tm, tk), lambda i, k: (i, k)),
                     pl.BlockSpec((tm, tk), lambda i, k: (i, k + nk)),
                     pl.BlockSpec((pr, tk), lambda i, k: (i, k)),
                     pl.BlockSpec((None, CONV_W, tk), lambda i, k: (l, 0, k)),
                     pl.BlockSpec((None, 1, tk), lambda i, k: (l, 0, k))]
        args += [gv, gv, p2, cw, cb.reshape(cb.shape[0], 1, k_dim)]
        conv_period = tm if period is None else period
    else:
        in_specs.append(pl.BlockSpec((tm, tk), lambda i, k: (i, k)))
        args.append(a)
        conv_period = 0
    w_mode = dict(pipeline_mode=pl.Buffered(1)) if nk == 1 else {}
    in_specs += [pl.BlockSpec((None, tk, n), lambda i, k: (l, k, 0), **w_mode),
                 pl.BlockSpec((tm, n), lambda i, k: (i, 0)),
                 gate.spec(tm, row),
                 pl.BlockSpec((None, 1, n), lambda i, k: (ln_idx, 0, 0)),
                 pl.BlockSpec((None, 1, n), lambda i, k: (ln_idx, 0, 0))]
    args += [w, resid, gate.arr, post_g.reshape(-1, 1, n), post_b.reshape(-1, 1, n)]
    for sc, sh in mods:
        in_specs += [sc.spec(tm, row), sh.spec(tm, row)]
        args += [sc.arr, sh.arr]
    out_shape = [jax.ShapeDtypeStruct((r, n), F32)] + [jax.ShapeDtypeStruct((r, n), BF)] * len(mods)
    out_specs = [pl.BlockSpec((tm, n), lambda i, k: (i, 0)) for _ in out_shape]
    res = pl.pallas_call(
        functools.partial(_down_kernel, conv_period=conv_period, n_mod=len(mods), nk=nk, kc=min(tk, CONV_COLS)),
        grid=(r // tm, nk),
        in_specs=in_specs,
        out_specs=out_specs,
        out_shape=out_shape,
        compiler_params=_cparams(2),
        name="mm_down_ln",
    )(*args)
    return res[0], list(res[1:])


def _dkv_kernel(x_ref, w_ref, g_ref, cs_ref, lat_ref, kr_ref):
    a = jnp.dot(x_ref[...], w_ref[...].astype(BF), preferred_element_type=F32)
    lat = a[:, :KV_LORA]
    lat_ref[...] = lat * lax.rsqrt(jnp.mean(lat * lat, axis=-1, keepdims=True) + EPS) * g_ref[...]
    t = a[:, KV_LORA:] * cs_ref[...]
    kr_ref[...] = t[:, :QK_ROPE] + t[:, QK_ROPE:]


def dkv(x, w_ext, norm_g, cs, cs_idx, tm):
    r, k = x.shape
    n = w_ext.shape[1]
    return pl.pallas_call(
        _dkv_kernel,
        grid=(r // tm,),
        in_specs=[pl.BlockSpec((tm, k), lambda i: (i, 0)),
                  pl.BlockSpec((k, n), lambda i: (0, 0)),
                  pl.BlockSpec((1, KV_LORA), lambda i: (0, 0)),
                  pl.BlockSpec((tm, 2 * QK_ROPE), lambda i: (cs_idx(i), 0))],
        out_specs=[pl.BlockSpec((tm, KV_LORA), lambda i: (i, 0)),
                   pl.BlockSpec((tm, QK_ROPE), lambda i: (i, 0))],
        out_shape=[jax.ShapeDtypeStruct((r, KV_LORA), F32), jax.ShapeDtypeStruct((r, QK_ROPE), F32)],
        compiler_params=_cparams(1),
        name="dkv",
    )(x, w_ext, norm_g.reshape(1, KV_LORA), cs)


def _flash_kernel(q_ref, kn_ref, kr_ref, v_ref, o_ref, *, tq):
    qi = pl.program_id(2)
    q = q_ref[...]

    def step(j, carry, masked):
        m, l, acc = carry
        ks = pl.multiple_of(j * tq, tq)
        kk = jnp.concatenate([kn_ref[pl.ds(ks, tq), :], kr_ref[pl.ds(ks, tq), :]], axis=1)
        s = lax.dot_general(q, kk, (((1,), (1,)), ((), ())), preferred_element_type=F32)
        if masked:
            row = lax.broadcasted_iota(jnp.int32, (tq, tq), 0)
            col = lax.broadcasted_iota(jnp.int32, (tq, tq), 1)
            s = jnp.where(row >= col, s, -jnp.inf)
        m_new = jnp.maximum(m, jnp.max(s, axis=-1, keepdims=True))
        alpha = jnp.exp2(m - m_new)
        p = jnp.exp2(s - m_new)
        l = alpha * l + jnp.sum(p, axis=-1, keepdims=True)
        acc = alpha * acc + jnp.dot(p.astype(BF), v_ref[pl.ds(ks, tq), :], preferred_element_type=F32)
        return m_new, l, acc

    init = (jnp.full((tq, 1), -jnp.inf, F32), jnp.zeros((tq, 1), F32), jnp.zeros((tq, V_HEAD), F32))
    carry = lax.fori_loop(0, qi, lambda j, c: step(j, c, False), init)
    _, l, acc = step(qi, carry, True)
    o_ref[...] = (acc / l).astype(o_ref.dtype)


def flash_prompt(q, kn, kr2, v, bsz, seq, tq):
    r = q.shape[0]
    nq = seq // tq
    return pl.pallas_call(
        functools.partial(_flash_kernel, tq=tq),
        grid=(bsz, N_HEADS, nq),
        in_specs=[pl.BlockSpec((tq, HEAD_W), lambda b, h, i: (b * nq + i, h)),
                  pl.BlockSpec((seq, QK_NOPE), lambda b, h, i: (b, h)),
                  pl.BlockSpec((seq, 2 * QK_ROPE), lambda b, h, i: (b, 0)),
                  pl.BlockSpec((seq, V_HEAD), lambda b, h, i: (b, h))],
        out_specs=pl.BlockSpec((tq, V_HEAD), lambda b, h, i: (b * nq + i, h)),
        out_shape=jax.ShapeDtypeStruct((r, N_HEADS * V_HEAD), BF),
        compiler_params=_cparams(3),
        name="flash_prompt",
    )(q, kn, kr2, v)


def _qlat_kernel(x_ref, w_ref, o_ref):
    x = x_ref[...]
    o_ref[:, :KV_LORA] = jnp.dot(x[:, :QK_NOPE], w_ref[...].astype(BF), preferred_element_type=F32)
    o_ref[:, KV_LORA:] = x[:, QK_NOPE:].astype(F32)


def q_to_latent(q, w_uk_t):
    r = q.shape[0]
    return pl.pallas_call(
        _qlat_kernel,
        grid=(N_HEADS,),
        in_specs=[pl.BlockSpec((r, HEAD_W), lambda h: (0, h)),
                  pl.BlockSpec((None, QK_NOPE, KV_LORA), lambda h: (h, 0, 0))],
        out_specs=pl.BlockSpec((None, r, KV_LORA + 2 * QK_ROPE), lambda h: (h, 0, 0)),
        out_shape=jax.ShapeDtypeStruct((N_HEADS, r, KV_LORA + 2 * QK_ROPE), F32),
        compiler_params=_cparams(1),
        name="q_to_latent",
    )(q, w_uk_t)


def _paged_kernel(pt_ref, q_ref, *refs, n_seq, n_pages, n_steps, t_new):
    n_in = n_seq * n_pages
    lat_refs = refs[:n_in]
    kr_refs = refs[n_in:2 * n_in]
    latn_ref, krn_ref, o_ref, ql_s, qr_s, m_s, l_s, acc_s = refs[2 * n_in:]
    jg = pl.program_id(1)
    rows = N_HEADS * t_new
    contract_last = (((1,), (1,)), ((), ()))

    @pl.when(jg == 0)
    def _():
        for sq in range(n_seq):
            q = q_ref[:, sq * t_new:(sq + 1) * t_new, :].reshape(rows, KV_LORA + 2 * QK_ROPE)
            ql_s[sq] = q[:, :KV_LORA].astype(BF)
            qr_s[sq] = (q[:, KV_LORA:KV_LORA + QK_ROPE] + q[:, KV_LORA + QK_ROPE:]).astype(BF)
            m_s[sq] = jnp.full((rows, 1), -jnp.inf, F32)
            l_s[sq] = jnp.zeros((rows, 1), F32)
            acc_s[sq] = jnp.zeros((rows, KV_LORA), F32)

    def update(sq, lat, s_rope, mask):
        s = lax.dot_general(ql_s[sq], lat, contract_last, preferred_element_type=F32) + s_rope
        if mask is not None:
            s = jnp.where(mask, s, -jnp.inf)
        m_old = m_s[sq]
        m_new = jnp.maximum(m_old, jnp.max(s, axis=-1, keepdims=True))
        alpha = jnp.exp2(m_old - m_new)
        p = jnp.exp2(s - m_new)
        l_s[sq] = alpha * l_s[sq] + jnp.sum(p, axis=-1, keepdims=True)
        acc_s[sq] = alpha * acc_s[sq] + jnp.dot(p.astype(BF), lat, preferred_element_type=F32)
        m_s[sq] = m_new

    for sq in range(n_seq):
        pages = slice(sq * n_pages, (sq + 1) * n_pages)
        lat = jnp.concatenate([r[...].astype(BF) for r in lat_refs[pages]], axis=0)
        kr_t = jnp.concatenate([r[...].astype(BF) for r in kr_refs[pages]], axis=1)
        update(sq, lat, jnp.dot(qr_s[sq], kr_t, preferred_element_type=F32), None)

    @pl.when(jg == n_steps - 1)
    def _():
        n_pad = latn_ref.shape[1]
        tq = lax.broadcasted_iota(jnp.int32, (rows, n_pad), 0) & (t_new - 1)
        tk = lax.broadcasted_iota(jnp.int32, (rows, n_pad), 1)
        for sq in range(n_seq):
            s_rope = lax.dot_general(qr_s[sq], krn_ref[sq], contract_last, preferred_element_type=F32)
            update(sq, latn_ref[sq], s_rope, tk <= tq)
            o_ref[:, sq * t_new:(sq + 1) * t_new, :] = (acc_s[sq] / l_s[sq]).reshape(N_HEADS, t_new, KV_LORA)


def paged_attention(q_lat, cache_lat, cache_kr_t, page_table, lat_new, kr_new, t_new):
    bsz, n_tbl = page_table.shape
    page = cache_lat.shape[1]
    npg, nsq = PAGES_PER_STEP, SEQS_PER_STEP
    n_steps = n_tbl // npg
    rows = N_HEADS * t_new
    n_pad = lat_new.shape[1]
    qw = KV_LORA + 2 * QK_ROPE

    def page_spec(shape, sq, p):
        return pl.BlockSpec((None,) + shape,
                            lambda b, j, pt: (pt[(b * nsq + sq) * n_tbl + j * npg + p], 0, 0))

    in_specs = [pl.BlockSpec((N_HEADS, nsq * t_new, qw), lambda b, j, pt: (0, b, 0))]
    in_specs += [page_spec((page, KV_LORA), sq, p) for sq in range(nsq) for p in range(npg)]
    in_specs += [page_spec((QK_ROPE, page), sq, p) for sq in range(nsq) for p in range(npg)]
    in_specs += [pl.BlockSpec((nsq, n_pad, KV_LORA), lambda b, j, pt: (b, 0, 0)),
                 pl.BlockSpec((nsq, n_pad, QK_ROPE), lambda b, j, pt: (b, 0, 0))]
    grid_spec = pltpu.PrefetchScalarGridSpec(
        num_scalar_prefetch=1,
        grid=(bsz // nsq, n_steps),
        in_specs=in_specs,
        out_specs=pl.BlockSpec((N_HEADS, nsq * t_new, KV_LORA), lambda b, j, pt: (0, b, 0)),
        scratch_shapes=[pltpu.VMEM((nsq, rows, KV_LORA), BF), pltpu.VMEM((nsq, rows, QK_ROPE), BF),
                        pltpu.VMEM((nsq, rows, 1), F32), pltpu.VMEM((nsq, rows, 1), F32),
                        pltpu.VMEM((nsq, rows, KV_LORA), F32)])
    n_in = nsq * npg
    return pl.pallas_call(
        functools.partial(_paged_kernel, n_seq=nsq, n_pages=npg, n_steps=n_steps, t_new=t_new),
        grid_spec=grid_spec,
        out_shape=jax.ShapeDtypeStruct((N_HEADS, bsz * t_new, KV_LORA), F32),
        compiler_params=_cparams(2),
        name="paged_attention",
    )(page_table.reshape(-1), q_lat, *([cache_lat] * n_in), *([cache_kr_t] * n_in), lat_new, kr_new)


def _ov_kernel(o_ref, w_ref, out_ref):
    out_ref[...] = jnp.dot(o_ref[...].astype(BF), w_ref[...].astype(BF),
                           preferred_element_type=F32).astype(out_ref.dtype)


def latent_to_v(o_lat, w_uv2d):
    r = o_lat.shape[1]
    return pl.pallas_call(
        _ov_kernel,
        grid=(N_HEADS,),
        in_specs=[pl.BlockSpec((None, r, KV_LORA), lambda h: (h, 0, 0)),
                  pl.BlockSpec((KV_LORA, V_HEAD), lambda h: (0, h))],
        out_specs=pl.BlockSpec((r, V_HEAD), lambda h: (0, h)),
        out_shape=jax.ShapeDtypeStruct((r, N_HEADS * V_HEAD), BF),
        compiler_params=_cparams(1),
        name="latent_to_v",
    )(o_lat, w_uv2d)


def _rope_swap(w):
    half = QK_ROPE // 2
    return jnp.concatenate([-w[..., half:], w[..., :half]], axis=-1)


def _rope_tables(pos):
    half = QK_ROPE // 2
    inv = jnp.power(ROPE_THETA, -jnp.arange(half, dtype=F32) / half)
    ang = pos.astype(F32)[:, None] * inv[None, :]
    cos, sin = jnp.cos(ang), jnp.sin(ang)
    cs = jnp.concatenate([cos, cos, sin, sin], axis=-1)
    q_tab = (SM_SCALE * LOG2E) * jnp.concatenate([jnp.ones((pos.shape[0], QK_NOPE), F32), cs], axis=-1)
    return cs, q_tab


def _trunk(x3, c_mods, kv_mod, pos, conv_state, paged, W, tm):
    bsz, t_rows, d = x3.shape
    r = bsz * t_rows
    x = x3.reshape(r, d)
    is_prompt = paged is None
    tiles_per_seq = max(t_rows // tm, 1)
    cs, q_tab = _rope_tables(pos)
    if is_prompt:
        tab_idx = lambda i: i % tiles_per_seq
    else:
        cs, q_tab = jnp.tile(cs, (tm // t_rows, 1)), jnp.tile(q_tab, (tm // t_rows, 1))
        tab_idx = lambda i: 0

    def vecs(mat, n):
        return [RowVec(mat[:, i * d:(i + 1) * d], t_rows) for i in range(n)]

    layer_mods = [vecs(m, 6) for m in c_mods]
    sc_kv_sh = vecs(kv_mod, 2)

    conv_new, v_rows = [], []
    latent = k_rope = None
    kv_ctx = None
    tm_d = 256 if is_prompt else 128
    tk_d = None
    h = modulate(x, layer_mods[0][1], layer_mods[0][0], 512 if is_prompt else 256)
    for l in range(DEPTH):
        sh1, sc1, g1, sh2, sc2, g2 = layer_mods[l]
        if l < N_A_LAYERS:
            z = mm(h, W["gm_w_in"], l, bias=W["gm_b_in"], act="gelu", tm=tm, tn=1024)
            if is_prompt:
                w_mix, b_t, bd = W["gm_w_s"][l], W["gm_b_s"][l].T, 0
            else:
                reps = CHUNK // t_rows
                w_mix = jnp.tile(W["gm_w_s"][l][:, :t_rows, :t_rows], (1, reps, reps))
                b_t = jnp.tile(W["gm_b_s"][l].T[:t_rows], (reps, 1))
                bd = t_rows
            gated, vn = spatial_gate(z, W["gm_ln_g"], W["gm_ln_b"], l, w_mix, b_t,
                                     block_diag=bd, write_v=not is_prompt)
            if vn is not None:
                v_rows.append(vn.reshape(bsz, t_rows, D_GMLP))
            x, (h,) = mm_down_ln(gated, W["gm_w_out"], l, x, g1, W["post_ln_g"], W["post_ln_b"],
                                 2 * l, [(sc2, sh2)], tm=tm_d, tk=tk_d)
        else:
            j = l - N_A_LAYERS
            cq = mm(h, W["w_dq"], j, rms_g=W["q_norm_g"], tm=tm, tn=Q_LORA)
            q = mm(cq, W["w_uq_ext"], j, tab=q_tab, tab_idx=tab_idx, tm=tm, tn=1024, sub=HEAD_W)
            if is_prompt:
                kn, kr2, v = kv_ctx
                o = flash_prompt(q, kn, kr2, v, bsz, t_rows, 1024)
            else:
                q_lat = q_to_latent(q, W["w_uk_t"])
                lat_new, kr_new = kv_ctx
                o_lat = paged_attention(q_lat, paged[0], paged[1], paged[2], lat_new, kr_new, t_rows)
                o = latent_to_v(o_lat, W["w_uv2d"])
            x, (h,) = mm_down_ln(o, W["w_o2d"], j, x, g1, W["post_ln_g"], W["post_ln_b"],
                                 2 * l, [(sc2, sh2)], tm=tm_d, tk=tk_d)
        gv = mm(h, W["ffn_w_up"], l, tm=tm, tn=1024)
        conv_new.append(gv.reshape(bsz, t_rows, 2 * D_FF)[:, t_rows - (CONV_W - 1):, :D_FF].astype(F32))
        mods = []
        if l + 1 < DEPTH:
            nsh1, nsc1 = layer_mods[l + 1][0], layer_mods[l + 1][1]
            mods.append((nsc1, nsh1))
        if l == N_A_LAYERS - 1:
            mods.append((sc_kv_sh[1], sc_kv_sh[0]))
        tm_f = tm_d
        if is_prompt:
            n_tiles = r // tm_f
            last2 = gv.reshape(n_tiles, tm_f, 2 * D_FF)[:, tm_f - 2:, :D_FF].astype(F32)
            prev2 = jnp.concatenate([jnp.zeros((1, 2, D_FF), F32), last2[:-1]], axis=0)
            starts = (jnp.arange(n_tiles) * tm_f) % t_rows == 0
            prev2 = jnp.where(starts[:, None, None], 0.0, prev2)
            period = None
        else:
            prev2 = conv_state[l].astype(F32)
            period = t_rows
        p2 = jnp.pad(prev2, ((0, 0), (0, 6), (0, 0))).reshape(-1, D_FF)
        x, hs = mm_down_ln(None, W["ffn_w_down"], l, x, g2, W["post_ln_g"], W["post_ln_b"], 2 * l + 1, mods,
                           conv=(gv, p2, W["ffn_conv_w"], W["ffn_conv_b"], period), tm=tm_f, tk=tk_d)
        if hs:
            h = hs[0]
        if l == N_A_LAYERS - 1:
            latent, k_rope = dkv(hs[1], W["w_dkv_ext"], W["kv_norm_g"], cs, tab_idx, tm)
            lat_bf = latent.astype(BF)
            if is_prompt:
                kn = mm(lat_bf, W["w_uk2d"][None], 0, tm=tm, tn=1024)
                v = mm(lat_bf, W["w_uv2d"][None], 0, tm=tm, tn=1024)
                kr2 = jnp.concatenate([k_rope, k_rope], axis=-1).astype(BF)
                kv_ctx = (kn, kr2, v)
            else:
                pad = ((0, 0), (0, 16 - t_rows), (0, 0))
                kv_ctx = (jnp.pad(lat_bf.reshape(bsz, t_rows, KV_LORA), pad),
                          jnp.pad(k_rope.astype(BF).reshape(bsz, t_rows, QK_ROPE), pad))
    y = x.reshape(bsz, t_rows, d)
    return (y, latent.reshape(bsz, t_rows, KV_LORA), k_rope.reshape(bsz, t_rows, QK_ROPE),
            jnp.stack(conv_new), jnp.stack(v_rows) if v_rows else None)


def kernel(x_prompt, x_sample, cache_kv_latent, cache_k_rope, state_conv, page_table, c_prompt, c_sample,
           ada_w, ada_b, post_ln_g, post_ln_b,
           gm_w_in, gm_b_in, gm_ln_g, gm_ln_b, gm_w_s, gm_b_s, gm_w_out,
           kv_ada_w, kv_ada_b, w_dkv, kv_norm_g, w_uk, w_uv,
           w_dq, q_norm_g, w_uq, w_o,
           ffn_w_up, ffn_conv_w, ffn_conv_b, ffn_w_down):
    d = D_MODEL
    n_b = w_dq.shape[0]
    w_uq_rope = w_uq[..., QK_NOPE:]
    w_uq_ext = jnp.concatenate([w_uq[..., :QK_NOPE], w_uq_rope, _rope_swap(w_uq_rope)], axis=-1)
    w_dkv_rope = w_dkv[:, KV_LORA:]
    W = dict(
        gm_w_in=gm_w_in, gm_b_in=gm_b_in, gm_ln_g=gm_ln_g, gm_ln_b=gm_ln_b, gm_w_s=gm_w_s, gm_b_s=gm_b_s,
        gm_w_out=gm_w_out.astype(BF), post_ln_g=post_ln_g.reshape(2 * DEPTH, d), post_ln_b=post_ln_b.reshape(2 * DEPTH, d),
        w_dq=w_dq, q_norm_g=q_norm_g,
        w_uq_ext=w_uq_ext.reshape(n_b, Q_LORA, N_HEADS * HEAD_W),
        w_o2d=w_o.reshape(n_b, N_HEADS * V_HEAD, d).astype(BF),
        w_dkv_ext=jnp.concatenate([w_dkv, _rope_swap(w_dkv_rope)], axis=-1),
        kv_norm_g=kv_norm_g,
        w_uk2d=w_uk.reshape(KV_LORA, N_HEADS * QK_NOPE),
        w_uv2d=w_uv.reshape(KV_LORA, N_HEADS * V_HEAD),
        w_uk_t=jnp.transpose(w_uk, (1, 2, 0)),
        ffn_w_up=ffn_w_up, ffn_conv_w=ffn_conv_w, ffn_conv_b=ffn_conv_b, ffn_w_down=ffn_w_down.astype(BF),
    )

    n_p, n_s = c_prompt.shape[0], c_sample.shape[0]
    n_c = -(-(n_p + n_s) // 8) * 8
    c_all = jnp.pad(jnp.concatenate([c_prompt, c_sample], axis=0), ((0, n_c - n_p - n_s), (0, 0)))
    mods = [mm(c_all, ada_w, l, bias=ada_b, x_act="silu", out_dtype=F32, tm=n_c, tn=1024) for l in range(DEPTH)]
    kv_mod = mm(c_all, kv_ada_w[None], 0, bias=kv_ada_b[None], x_act="silu", out_dtype=F32, tm=n_c, tn=1024)

    seq_p = x_prompt.shape[1]
    y_p, lat_p, kr_p, conv_p, _ = _trunk(
        x_prompt, [m[:n_p] for m in mods], kv_mod[:n_p], jnp.arange(seq_p, dtype=jnp.int32),
        None, None, W, tm=2048)

    past = page_table.shape[1] * cache_kv_latent.shape[1]
    seq_s = x_sample.shape[1]
    y_s, lat_s, kr_s, conv_s, v_s = _trunk(
        x_sample, [m[n_p:n_p + n_s] for m in mods], kv_mod[n_p:n_p + n_s],
        past + jnp.arange(seq_s, dtype=jnp.int32), state_conv,
        (cache_kv_latent, jnp.swapaxes(cache_k_rope, 1, 2), page_table), W, tm=1024)

    return (y_p, y_s, lat_p, kr_p, lat_s, kr_s, conv_p, conv_s, v_s)
```
